```python
import jax, jax.numpy as jnp
from jax import lax
import numpy as np

D_MODEL = 1024
BATCH = 1
SEQ = 16384
DEPTH = 2
DEC_BATCH = 128
DEC_SEQ = 1
PAST_LEN = 16384
PAGE_SIZE = 128

CHUNK = 128
D_A = D_MODEL
A_GROUPS = 8
A_GROUP_DIM = D_A // A_GROUPS
D_B = D_MODEL
CONV_W = 3
N_HEADS = 16
N_KV_HEADS = 4
HEAD_DIM = 64
Q_PER_KV = N_HEADS // N_KV_HEADS
WINDOW = 128
ATT_BLOCK = WINDOW
ROPE_THETA = 10000.0
N_BRANCH = 3
D_FF = ((-(-8 * D_MODEL // 3) + 255) // 256) * 256
IN_SIZES = (D_A, D_A, D_B, D_B, D_B, N_HEADS * HEAD_DIM, N_KV_HEADS * HEAD_DIM, N_KV_HEADS * HEAD_DIM, N_BRANCH * D_MODEL)
IN_WIDTH = sum(IN_SIZES)
NEG_INF = -1e30

kernel_name = 'hybrid_gated_chunkmlp_shortconv_swa_step'


def split_cols(t, sizes):
    idx = [int(i) for i in np.cumsum(sizes)[:-1]]
    return jnp.split(t, idx, axis=-1)


def rmsnorm(x, g, eps=1e-6):
    xf = x.astype(jnp.float32)
    y = xf * lax.rsqrt(jnp.mean(xf * xf, axis=-1, keepdims=True) + eps)
    return (y * g.astype(jnp.float32)).astype(x.dtype)


def layernorm(x, g, b, eps=1e-5):
    xf = x.astype(jnp.float32)
    mu = jnp.mean(xf, axis=-1, keepdims=True)
    var = jnp.mean(jnp.square(xf - mu), axis=-1, keepdims=True)
    y = (xf - mu) * lax.rsqrt(var + eps) * g.astype(jnp.float32) + b.astype(jnp.float32)
    return y.astype(x.dtype)


def rope(x, pos):
    half = HEAD_DIM // 2
    inv = jnp.power(jnp.float32(ROPE_THETA), -jnp.arange(half, dtype=jnp.float32) * (2.0 / HEAD_DIM))
    ang = pos.astype(jnp.float32)[:, None] * inv[None, :]
    cos = jnp.cos(ang)[:, None, :]
    sin = jnp.sin(ang)[:, None, :]
    xf = x.astype(jnp.float32)
    x1, x2 = xf[..., :half], xf[..., half:]
    return jnp.concatenate([x1 * cos - x2 * sin, x2 * cos + x1 * sin], axis=-1).astype(x.dtype)


def chunk_spatial(v, w_s, b_s, n):
    tril = jnp.tril(jnp.ones((n, n), dtype=bool))
    w = jnp.where(tril[None], w_s[:, :n, :n], 0.0).astype(v.dtype)
    out = jnp.einsum('gts,bcsgd->bctgd', w, v)
    return out + jnp.transpose(b_s[:, :n])[None, None, :, :, None].astype(v.dtype)


def causal_conv(xp, w, b):
    L = xp.shape[1] - (CONV_W - 1)
    y = b[None, None, :]
    for j in range(CONV_W):
        y = y + w[j][None, None, :] * xp[:, j:j + L]
    return y


def with_prev_block(t):
    prev = jnp.concatenate([jnp.zeros_like(t[:, :1]), t[:, :-1]], axis=1)
    return jnp.concatenate([prev, t], axis=2)


def sink_attention(q, k, v, mask, sinks):
    s = jnp.einsum('bnqgrd,bnkgd->bngrqk', q, k).astype(jnp.float32) * (HEAD_DIM ** -0.5)
    s = jnp.where(mask[None, :, None, None], s, NEG_INF)
    sk = sinks.astype(jnp.float32).reshape(N_KV_HEADS, Q_PER_KV)[None, None, :, :, None, None]
    m = jnp.maximum(jnp.max(s, axis=-1, keepdims=True), sk)
    e = jnp.exp(s - m)
    p = e / (jnp.sum(e, axis=-1, keepdims=True) + jnp.exp(sk - m))
    return jnp.einsum('bngrqk,bnkgd->bnqgrd', p.astype(v.dtype), v)


def trunk_layer(x, lp, prompt, conv_hist, win_k, win_v):
    bsz, L, _ = x.shape
    h = rmsnorm(x, lp['norm_pre_mix'])
    proj = h @ lp['w_in']
    ua, va, bg, cg, hb, q, k, v, gates = split_cols(proj, IN_SIZES)

    ua = jax.nn.gelu(ua)
    va = layernorm(jax.nn.gelu(va), lp['chunk_ln_g'], lp['chunk_ln_b'])
    n = CHUNK if prompt else L
    nc = L // n
    sa = chunk_spatial(va.reshape(bsz, nc, n, A_GROUPS, A_GROUP_DIM), lp['w_spatial'], lp['b_spatial'], n)
    y_a = (ua * sa.reshape(bsz, L, D_A)) @ lp['w_br_a']

    cx = cg * hb
    hist = jnp.zeros((bsz, CONV_W - 1, D_B), x.dtype) if prompt else conv_hist.astype(x.dtype)
    cxp = jnp.concatenate([hist, cx], axis=1)
    y_b = (bg * causal_conv(cxp, lp['conv_w'].astype(x.dtype), lp['conv_b'].astype(x.dtype))) @ lp['w_br_b']
    new_conv = cxp[:, -(CONV_W - 1):]

    pos = jnp.arange(L, dtype=jnp.int32) if prompt else PAST_LEN + jnp.arange(L, dtype=jnp.int32)
    q = rope(q.reshape(bsz, L, N_HEADS, HEAD_DIM), pos)
    k = rope(k.reshape(bsz, L, N_KV_HEADS, HEAD_DIM), pos)
    v = v.reshape(bsz, L, N_KV_HEADS, HEAD_DIM)
    if prompt:
        nb = L // ATT_BLOCK
        qb = q.reshape(bsz, nb, ATT_BLOCK, N_KV_HEADS, Q_PER_KV, HEAD_DIM)
        kk = with_prev_block(k.reshape(bsz, nb, ATT_BLOCK, N_KV_HEADS, HEAD_DIM))
        vv = with_prev_block(v.reshape(bsz, nb, ATT_BLOCK, N_KV_HEADS, HEAD_DIM))
        qpos = pos.reshape(nb, ATT_BLOCK)
        kpos = jnp.concatenate([qpos - ATT_BLOCK, qpos], axis=1)
        new_k = k[:, -WINDOW:]
        new_v = v[:, -WINDOW:]
    else:
        qb = q.reshape(bsz, 1, L, N_KV_HEADS, Q_PER_KV, HEAD_DIM)
        kfull = jnp.concatenate([win_k.astype(x.dtype), k], axis=1)
        vfull = jnp.concatenate([win_v.astype(x.dtype), v], axis=1)
        kk = kfull[:, None]
        vv = vfull[:, None]
        qpos = pos[None]
        kpos = (PAST_LEN - WINDOW + jnp.arange(WINDOW + L, dtype=jnp.int32))[None]
        new_k = kfull[:, -WINDOW:]
        new_v = vfull[:, -WINDOW:]
    diff = qpos[:, :, None] - kpos[:, None, :]
    mask = (diff >= 0) & (diff <= WINDOW) & (kpos[:, None, :] >= 0)
    o = sink_attention(qb, kk, vv, mask, lp['attn_sinks'])
    y_c = o.reshape(bsz, L, N_HEADS * HEAD_DIM) @ lp['w_br_c']

    g_a, g_b, g_c = jnp.split(jax.nn.sigmoid(gates), N_BRANCH, axis=-1)
    merged = g_a * y_a + g_b * y_b + g_c * y_c
    x = x + rmsnorm(merged @ lp['w_out'], lp['norm_post_mix'])

    h2 = rmsnorm(x, lp['norm_pre_ffn'])
    f = (jax.nn.silu(h2 @ lp['w_ffn_gate']) * (h2 @ lp['w_ffn_up'])) @ lp['w_ffn_down']
    x = x + rmsnorm(f, lp['norm_post_ffn'])
    return x, new_conv, new_k, new_v, va


def setup_inputs(seed: int = 0) -> dict:
    key = jax.random.key(seed)
    ks = jax.random.split(key, 32)
    f32 = jnp.float32

    def nrm(k, shape, scale):
        return jax.random.normal(k, shape, f32) * scale

    def gain(k, shape):
        return 1.0 + 0.05 * jax.random.normal(k, shape, f32)

    kv_shape = (DEPTH, DEC_BATCH, WINDOW, N_KV_HEADS, HEAD_DIM)
    return {
        'x_prompt': nrm(ks[0], (BATCH, SEQ, D_MODEL), 1.0),
        'x_sample': nrm(ks[1], (DEC_BATCH, DEC_SEQ, D_MODEL), 1.0),
        'state_conv': nrm(ks[2], (DEPTH, DEC_BATCH, CONV_W - 1, D_B), 1.0),
        'cache_win_k': nrm(ks[3], kv_shape, 1.0),
        'cache_win_v': nrm(ks[4], kv_shape, 1.0),
        'norm_pre_mix': gain(ks[5], (DEPTH, D_MODEL)),
        'norm_post_mix': gain(ks[6], (DEPTH, D_MODEL)),
        'norm_pre_ffn': gain(ks[7], (DEPTH, D_MODEL)),
        'norm_post_ffn': gain(ks[8], (DEPTH, D_MODEL)),
        'w_in': nrm(ks[9], (DEPTH, D_MODEL, IN_WIDTH), D_MODEL ** -0.5),
        'chunk_ln_g': gain(ks[10], (DEPTH, D_A)),
        'chunk_ln_b': nrm(ks[11], (DEPTH, D_A), 0.02),
        'w_spatial': nrm(ks[12], (DEPTH, A_GROUPS, CHUNK, CHUNK), CHUNK ** -0.5),
        'b_spatial': 1.0 + nrm(ks[13], (DEPTH, A_GROUPS, CHUNK), 0.1),
        'conv_w': nrm(ks[14], (DEPTH, CONV_W, D_B), CONV_W ** -0.5),
        'conv_b': nrm(ks[15], (DEPTH, D_B), 0.02),
        'attn_sinks': nrm(ks[16], (DEPTH, N_HEADS), 0.5),
        'w_br_a': nrm(ks[17], (DEPTH, D_A, D_MODEL), D_A ** -0.5),
        'w_br_b': nrm(ks[18], (DEPTH, D_B, D_MODEL), D_B ** -0.5),
        'w_br_c': nrm(ks[19], (DEPTH, N_HEADS * HEAD_DIM, D_MODEL), (N_HEADS * HEAD_DIM) ** -0.5),
        'w_out': nrm(ks[20], (DEPTH, D_MODEL, D_MODEL), D_MODEL ** -0.5),
        'w_ffn_gate': nrm(ks[21], (DEPTH, D_MODEL, D_FF), D_MODEL ** -0.5),
        'w_ffn_up': nrm(ks[22], (DEPTH, D_MODEL, D_FF), D_MODEL ** -0.5),
        'w_ffn_down': nrm(ks[23], (DEPTH, D_FF, D_MODEL), D_FF ** -0.5),
    }


def reference(x_prompt, x_sample, state_conv, cache_win_k, cache_win_v,
              norm_pre_mix, norm_post_mix, norm_pre_ffn, norm_post_ffn, w_in,
              chunk_ln_g, chunk_ln_b, w_spatial, b_spatial, conv_w, conv_b, attn_sinks,
              w_br_a, w_br_b, w_br_c, w_out, w_ffn_gate, w_ffn_up, w_ffn_down):
    xp, xs = x_prompt, x_sample
    p_conv, p_k, p_v = [], [], []
    s_conv, s_k, s_v, s_cv = [], [], [], []
    for l in range(DEPTH):
        lp = {
            'norm_pre_mix': norm_pre_mix[l], 'norm_post_mix': norm_post_mix[l],
            'norm_pre_ffn': norm_pre_ffn[l], 'norm_post_ffn': norm_post_ffn[l],
            'w_in': w_in[l], 'chunk_ln_g': chunk_ln_g[l], 'chunk_ln_b': chunk_ln_b[l],
            'w_spatial': w_spatial[l], 'b_spatial': b_spatial[l],
            'conv_w': conv_w[l], 'conv_b': conv_b[l], 'attn_sinks': attn_sinks[l],
            'w_br_a': w_br_a[l], 'w_br_b': w_br_b[l], 'w_br_c': w_br_c[l], 'w_out': w_out[l],
            'w_ffn_gate': w_ffn_gate[l], 'w_ffn_up': w_ffn_up[l], 'w_ffn_down': w_ffn_down[l],
        }
        xp, c, k, v, _ = trunk_layer(xp, lp, True, None, None, None)
        p_conv.append(c)
        p_k.append(k)
        p_v.append(v)
        xs, c2, k2, v2, cv = trunk_layer(xs, lp, False, state_conv[l], cache_win_k[l], cache_win_v[l])
        s_conv.append(c2)
        s_k.append(k2)
        s_v.append(v2)
        s_cv.append(cv)
    return (xp, xs, jnp.stack(p_conv), jnp.stack(p_k), jnp.stack(p_v),
            jnp.stack(s_conv), jnp.stack(s_k), jnp.stack(s_v), jnp.stack(s_cv))
```

```python
import functools

import jax
import jax.numpy as jnp
import numpy as np
from jax.experimental import pallas as pl
from jax.experimental.pallas import tpu as pltpu

D_MODEL = 1024
SEQ = 16384
DEPTH = 2
DEC_BATCH = 128
PAST_LEN = 16384
CHUNK = 128
A_GROUPS = 8
A_GROUP_DIM = D_MODEL // A_GROUPS
CONV_W = 3
N_HEADS = 16
N_KV_HEADS = 4
HEAD_DIM = 64
Q_PER_KV = N_HEADS // N_KV_HEADS
WINDOW = 128
ROPE_THETA = 10000.0
D_FF = 2816
KV_DIM = N_KV_HEADS * HEAD_DIM
NEG_INF = -1e30

OFF_UA = 0
OFF_VA = OFF_UA + D_MODEL
OFF_BG = OFF_VA + D_MODEL
OFF_CG = OFF_BG + D_MODEL
OFF_HB = OFF_CG + D_MODEL
OFF_Q = OFF_HB + D_MODEL
OFF_K = OFF_Q + N_HEADS * HEAD_DIM
OFF_V = OFF_K + KV_DIM
OFF_GA = OFF_V + KV_DIM
OFF_GB = OFF_GA + D_MODEL
OFF_GC = OFF_GB + D_MODEL
IN_WIDTH = OFF_GC + D_MODEL

LANES = 128
V7X_VMEM_LIMIT = 56 * 1024 * 1024

TM_MIX = 256
TM_FFN = 512
BLK = WINDOW
S2_BB = 16

assert CHUNK == BLK and WINDOW == BLK and PAST_LEN >= WINDOW
assert 2 * HEAD_DIM == LANES and A_GROUP_DIM == LANES

BF16 = jnp.bfloat16
F32 = jnp.float32


def _dot(a, b):
    return jnp.dot(a, b, preferred_element_type=F32)


def _dot_nt(a, b):
    return jax.lax.dot_general(a, b, (((1,), (1,)), ((), ())), preferred_element_type=F32)


def _rmsnorm(x, g, eps=1e-6):
    return x * jax.lax.rsqrt(jnp.mean(x * x, axis=-1, keepdims=True) + eps) * g


def _layernorm(x, g, b, eps=1e-5):
    mu = jnp.mean(x, axis=-1, keepdims=True)
    xc = x - mu
    var = jnp.mean(xc * xc, axis=-1, keepdims=True)
    return xc * jax.lax.rsqrt(var + eps) * g + b


def _lane_iota(shape):
    return jax.lax.broadcasted_iota(jnp.int32, shape, len(shape) - 1)


def _rope_block(xb, cos_t, sin_s):
    lane = _lane_iota(xb.shape)
    first_half = (lane % HEAD_DIM) < (HEAD_DIM // 2)
    rot = jnp.where(first_half,
                    pltpu.roll(xb, LANES - HEAD_DIM // 2, axis=1),
                    pltpu.roll(xb, HEAD_DIM // 2, axis=1))
    return xb * cos_t + rot * sin_s


def _rope(x, cos_t, sin_s):
    n = x.shape[1] // LANES
    return jnp.concatenate(
        [_rope_block(x[:, j * LANES:(j + 1) * LANES], cos_t, sin_s) for j in range(n)], axis=1)


def _split_kv_heads(blk):
    out = []
    for j in range(KV_DIM // LANES):
        b = blk[:, j * LANES:(j + 1) * LANES]
        r = pltpu.roll(b, HEAD_DIM, axis=1)
        low = _lane_iota(b.shape) < HEAD_DIM
        zero = jnp.zeros_like(b)
        out.append((jnp.where(low, b, zero).astype(BF16), jnp.where(low, zero, r).astype(BF16)))
        out.append((jnp.where(low, r, zero).astype(BF16), jnp.where(low, zero, b).astype(BF16)))
    return out


def _mixer_kernel(sinks_ref, x_ref, cos_ref, sin_ref, gpre_ref, gpost_ref, w_in_ref,
                  lng_ref, lnb_ref, wsp_ref, bsp_ref, convw_ref, convb_ref,
                  wa_ref, wb_ref, wc_ref, wo_ref,
                  x1_ref, cxlast_ref, klast_ref, vlast_ref,
                  wsp_s, kprev_s, vprev_s, cxprev_s):
    step = pl.program_id(0)
    nblk = TM_MIX // BLK

    @pl.when(step == 0)
    def _init():
        row = jax.lax.broadcasted_iota(jnp.int32, (CHUNK, CHUNK), 0)
        col = jax.lax.broadcasted_iota(jnp.int32, (CHUNK, CHUNK), 1)
        for g in range(A_GROUPS):
            wsp_s[g] = jnp.where(col <= row, wsp_ref[g], 0.0).astype(BF16)
        kprev_s[...] = jnp.zeros_like(kprev_s)
        vprev_s[...] = jnp.zeros_like(vprev_s)
        cxprev_s[...] = jnp.zeros_like(cxprev_s)

    x = x_ref[...]
    h = _rmsnorm(x, gpre_ref[...]).astype(BF16)

    def proj(off, width):
        return _dot(h, w_in_ref[:, off:off + width])

    ua = jax.nn.gelu(proj(OFF_UA, D_MODEL))
    va = _layernorm(jax.nn.gelu(proj(OFF_VA, D_MODEL)), lng_ref[...], lnb_ref[...]).astype(BF16)
    sa_rows = []
    for c in range(nblk):
        sa_rows.append(jnp.concatenate(
            [_dot(wsp_s[g], va[c * BLK:(c + 1) * BLK, g * LANES:(g + 1) * LANES])
             for g in range(A_GROUPS)], axis=1) + bsp_ref[...])
    sa = jnp.concatenate(sa_rows, axis=0)
    y_a = _dot((ua * sa).astype(BF16), wa_ref[...])
    merged = jax.nn.sigmoid(proj(OFF_GA, D_MODEL)) * y_a

    cx = proj(OFF_CG, D_MODEL) * proj(OFF_HB, D_MODEL)
    prev = cxprev_s[...]
    rowi = jax.lax.broadcasted_iota(jnp.int32, cx.shape, 0)
    cx1 = jnp.where(rowi < 1, prev[7:8, :], pltpu.roll(cx, 1, axis=0))
    cx2 = jnp.where(rowi < 2, jnp.where(rowi < 1, prev[6:7, :], prev[7:8, :]),
                    pltpu.roll(cx, 2, axis=0))
    conv = convb_ref[...] + convw_ref[0:1, :] * cx2
    conv = conv + convw_ref[1:2, :] * cx1
    conv = conv + convw_ref[2:3, :] * cx
    cxprev_s[...] = cx[TM_MIX - 8:, :]
    cxlast_ref[...] = cx[TM_MIX - 8:, :]
    y_b = _dot((proj(OFF_BG, D_MODEL) * conv).astype(BF16), wb_ref[...])
    merged = merged + jax.nn.sigmoid(proj(OFF_GB, D_MODEL)) * y_b

    cos_t = cos_ref[...]
    sin_s = sin_ref[...]
    q = (_rope(proj(OFF_Q, N_HEADS * HEAD_DIM), cos_t, sin_s) * (HEAD_DIM ** -0.5)).astype(BF16)
    k = _rope(proj(OFF_K, KV_DIM), cos_t, sin_s)
    v = proj(OFF_V, KV_DIM)
    klast_ref[...] = k[TM_MIX - BLK:, :]
    vlast_ref[...] = v[TM_MIX - BLK:, :]

    qi = jax.lax.broadcasted_iota(jnp.int32, (BLK, 2 * BLK), 0)
    kj = jax.lax.broadcasted_iota(jnp.int32, (BLK, 2 * BLK), 1)
    band = (kj >= qi) & (kj <= qi + WINDOW)
    lane_low = _lane_iota((BLK, LANES)) < HEAD_DIM

    k_prev = [(kprev_s[g, 0], kprev_s[g, 1]) for g in range(N_KV_HEADS)]
    v_prev = [(vprev_s[g, 0], vprev_s[g, 1]) for g in range(N_KV_HEADS)]
    o_rows = []
    for c in range(nblk):
        rows = slice(c * BLK, (c + 1) * BLK)
        k_cur = _split_kv_heads(k[rows, :])
        v_cur = _split_kv_heads(v[rows, :])
        if c == 0:
            mask = band & (kj >= jnp.where(step == 0, BLK, 0))
        else:
            mask = band
        o_pairs = []
        for m in range(N_HEADS // 2):
            g = (2 * m) // Q_PER_KV
            kbd = jnp.concatenate([k_prev[g][0], k_cur[g][0], k_prev[g][1], k_cur[g][1]], axis=0)
            vbd = jnp.concatenate([v_prev[g][0], v_cur[g][0], v_prev[g][1], v_cur[g][1]], axis=0)
            s = _dot_nt(q[rows, m * LANES:(m + 1) * LANES], kbd)
            es, invs = [], []
            for hh in range(2):
                sink = sinks_ref[2 * m + hh]
                sh = jnp.where(mask, s[:, hh * 2 * BLK:(hh + 1) * 2 * BLK], NEG_INF)
                mx = jnp.maximum(jnp.max(sh, axis=-1, keepdims=True), sink)
                e = jnp.exp(sh - mx)
                den = jnp.sum(e, axis=-1, keepdims=True) + jnp.exp(sink - mx)
                es.append(e.astype(BF16))
                invs.append(1.0 / den)
            o_pair = _dot(jnp.concatenate(es, axis=1), vbd)
            o_pairs.append(o_pair * jnp.where(lane_low, invs[0], invs[1]))
        o_rows.append(jnp.concatenate(o_pairs, axis=1))
        k_prev, v_prev = k_cur, v_cur
    for g in range(N_KV_HEADS):
        kprev_s[g, 0] = k_prev[g][0]
        kprev_s[g, 1] = k_prev[g][1]
        vprev_s[g, 0] = v_prev[g][0]
        vprev_s[g, 1] = v_prev[g][1]
    o = jnp.concatenate(o_rows, axis=0).astype(BF16)
    y_c = _dot(o, wc_ref[...])
    merged = merged + jax.nn.sigmoid(proj(OFF_GC, D_MODEL)) * y_c

    x1_ref[...] = x + _rmsnorm(_dot(merged.astype(BF16), wo_ref[...]), gpost_ref[...])


def _resident(shape):
    nd = len(shape)
    return pl.BlockSpec(shape, lambda *_: (0,) * nd, pipeline_mode=pl.Buffered(1))


def _prompt_mixer(x, cos_t, sin_s, sinks, gpre, gpost, w_in, lng, lnb, wsp, bsp, convw, convb,
                  wa, wb, wc, wo):
    n = x.shape[0]
    grid = (n // TM_MIX,)
    row_tile = lambda w: pl.BlockSpec((TM_MIX, w), lambda i, *_: (i, 0))
    in_specs = [
        row_tile(D_MODEL), row_tile(LANES), row_tile(LANES),
        _resident((1, D_MODEL)), _resident((1, D_MODEL)),
        _resident((D_MODEL, IN_WIDTH)),
        _resident((1, D_MODEL)), _resident((1, D_MODEL)),
        _resident((A_GROUPS, CHUNK, CHUNK)), _resident((CHUNK, D_MODEL)),
        _resident((CONV_W, D_MODEL)), _resident((1, D_MODEL)),
        _resident((D_MODEL, D_MODEL)), _resident((D_MODEL, D_MODEL)),
        _resident((D_MODEL, D_MODEL)), _resident((D_MODEL, D_MODEL)),
    ]
    out_specs = [
        row_tile(D_MODEL),
        pl.BlockSpec((8, D_MODEL), lambda i, *_: (0, 0)),
        pl.BlockSpec((BLK, KV_DIM), lambda i, *_: (0, 0)),
        pl.BlockSpec((BLK, KV_DIM), lambda i, *_: (0, 0)),
    ]
    out_shape = [
        jax.ShapeDtypeStruct((n, D_MODEL), F32),
        jax.ShapeDtypeStruct((8, D_MODEL), F32),
        jax.ShapeDtypeStruct((BLK, KV_DIM), F32),
        jax.ShapeDtypeStruct((BLK, KV_DIM), F32),
    ]
    scratch = [
        pltpu.VMEM((A_GROUPS, CHUNK, CHUNK), BF16),
        pltpu.VMEM((N_KV_HEADS, 2, BLK, LANES), BF16),
        pltpu.VMEM((N_KV_HEADS, 2, BLK, LANES), BF16),
        pltpu.VMEM((8, D_MODEL), F32),
    ]
    return pl.pallas_call(
        _mixer_kernel,
        grid_spec=pltpu.PrefetchScalarGridSpec(
            num_scalar_prefetch=1, grid=grid, in_specs=in_specs, out_specs=out_specs,
            scratch_shapes=scratch),
        out_shape=out_shape,
        compiler_params=pltpu.CompilerParams(
            dimension_semantics=("arbitrary",), vmem_limit_bytes=V7X_VMEM_LIMIT),
        name="prompt_mixer",
    )(sinks, x, cos_t, sin_s, gpre, gpost, w_in, lng, lnb, wsp, bsp, convw, convb, wa, wb, wc, wo)


def _swiglu(x, gpre, gpost, wg_ref, wu_ref, wd_ref):
    h = _rmsnorm(x, gpre).astype(BF16)
    a = jax.nn.silu(_dot(h, wg_ref[...])) * _dot(h, wu_ref[...])
    return x + _rmsnorm(_dot(a.astype(BF16), wd_ref[...]), gpost)


def _ffn_kernel(x_ref, gpre_ref, gpost_ref, wg_ref, wu_ref, wd_ref, y_ref):
    y_ref[...] = _swiglu(x_ref[...], gpre_ref[...], gpost_ref[...], wg_ref, wu_ref, wd_ref)


def _prompt_ffn(x, gpre, gpost, wg, wu, wd):
    n = x.shape[0]
    row_tile = pl.BlockSpec((TM_FFN, D_MODEL), lambda i: (i, 0))
    return pl.pallas_call(
        _ffn_kernel,
        grid=(n // TM_FFN,),
        in_specs=[row_tile, _resident((1, D_MODEL)), _resident((1, D_MODEL)),
                  _resident((D_MODEL, D_FF)), _resident((D_MODEL, D_FF)),
                  _resident((D_FF, D_MODEL))],
        out_specs=row_tile,
        out_shape=jax.ShapeDtypeStruct((n, D_MODEL), F32),
        compiler_params=pltpu.CompilerParams(
            dimension_semantics=("parallel",), vmem_limit_bytes=V7X_VMEM_LIMIT),
        name="prompt_ffn",
    )(x, gpre, gpost, wg, wu, wd)


def _sample_proj_kernel(x_ref, hist_ref, cos_ref, sin_ref, gpre_ref, w_in_ref, lng_ref, lnb_ref,
                        wsp0_ref, bsp0_ref, convw_ref, convb_ref, wa_ref, wb_ref,
                        va_ref, newconv_ref, q_ref, k_ref, v_ref, mab_ref, gc_ref):
    x = x_ref[...]
    h = _rmsnorm(x, gpre_ref[...]).astype(BF16)

    def proj(off, width):
        return _dot(h, w_in_ref[:, off:off + width])

    ua = jax.nn.gelu(proj(OFF_UA, D_MODEL))
    va = _layernorm(jax.nn.gelu(proj(OFF_VA, D_MODEL)), lng_ref[...], lnb_ref[...])
    va_ref[...] = va
    sa = wsp0_ref[...] * va + bsp0_ref[...]
    y_a = _dot((ua * sa).astype(BF16), wa_ref[...])
    merged = jax.nn.sigmoid(proj(OFF_GA, D_MODEL)) * y_a

    cx = proj(OFF_CG, D_MODEL) * proj(OFF_HB, D_MODEL)
    h0 = hist_ref[:, 0:D_MODEL]
    h1 = hist_ref[:, D_MODEL:2 * D_MODEL]
    conv = convb_ref[...] + convw_ref[0:1, :] * h0
    conv = conv + convw_ref[1:2, :] * h1
    conv = conv + convw_ref[2:3, :] * cx
    newconv_ref[:, 0:D_MODEL] = h1
    newconv_ref[:, D_MODEL:2 * D_MODEL] = cx
    y_b = _dot((proj(OFF_BG, D_MODEL) * conv).astype(BF16), wb_ref[...])
    mab_ref[...] = merged + jax.nn.sigmoid(proj(OFF_GB, D_MODEL)) * y_b
    gc_ref[...] = jax.nn.sigmoid(proj(OFF_GC, D_MODEL))

    cos_t = cos_ref[...]
    sin_s = sin_ref[...]
    q_ref[...] = _rope(proj(OFF_Q, N_HEADS * HEAD_DIM), cos_t, sin_s) * (HEAD_DIM ** -0.5)
    k_ref[...] = _rope(proj(OFF_K, KV_DIM), cos_t, sin_s)
    v_ref[...] = proj(OFF_V, KV_DIM)


def _sample_proj(x, hist, cos_t, sin_s, gpre, w_in, lng, lnb, wsp0, bsp0, convw, convb, wa, wb):
    b = x.shape[0]
    full = lambda shape: _resident(shape)
    in_arrays = (x, hist, cos_t, sin_s, gpre, w_in, lng, lnb, wsp0, bsp0, convw, convb, wa, wb)
    out_shape = [
        jax.ShapeDtypeStruct((b, D_MODEL), F32),
        jax.ShapeDtypeStruct((b, 2 * D_MODEL), F32),
        jax.ShapeDtypeStruct((b, D_MODEL), F32),
        jax.ShapeDtypeStruct((b, KV_DIM), F32),
        jax.ShapeDtypeStruct((b, KV_DIM), F32),
        jax.ShapeDtypeStruct((b, D_MODEL), F32),
        jax.ShapeDtypeStruct((b, D_MODEL), F32),
    ]
    return pl.pallas_call(
        _sample_proj_kernel,
        grid=(1,),
        in_specs=[full(a.shape) for a in in_arrays],
        out_specs=[pl.BlockSpec(s.shape, lambda i: (0, 0)) for s in out_shape],
        out_shape=out_shape,
        compiler_params=pltpu.CompilerParams(
            dimension_semantics=("arbitrary",), vmem_limit_bytes=V7X_VMEM_LIMIT),
        name="sample_proj",
    )(*in_arrays)


def _sample_attn_kernel(sinks_ref, qx_ref, knew_ref, vnew_ref, kc_ref, vc_ref,
                        o_ref, kout_ref, vout_ref):
    bb = qx_ref.shape[0]
    head = jax.lax.broadcasted_iota(jnp.int32, (N_HEADS, KV_DIM), 0)
    lane = jax.lax.broadcasted_iota(jnp.int32, (N_HEADS, KV_DIM), 1)
    own = (lane // HEAD_DIM) == (head // Q_PER_KV)
    qf = jnp.where(own[None], qx_ref[...], 0.0)
    qexp = qf.astype(BF16)
    kc = kc_ref[...]
    vc = vc_ref[...]
    knew = knew_ref[...]
    vnew = vnew_ref[...]

    s = jnp.einsum('bhc,bkc->bhk', qexp, kc.astype(BF16), preferred_element_type=F32)
    s_new = jnp.sum(qf * knew, axis=-1, keepdims=True)
    hrow = jax.lax.broadcasted_iota(jnp.int32, (1, N_HEADS, 1), 1)
    sink = jnp.zeros((1, N_HEADS, 1), F32)
    for hh in range(N_HEADS):
        sink = jnp.where(hrow == hh, sinks_ref[hh], sink)
    mx = jnp.maximum(jnp.maximum(jnp.max(s, axis=-1, keepdims=True), s_new), sink)
    e = jnp.exp(s - mx)
    e_new = jnp.exp(s_new - mx)
    den = jnp.sum(e, axis=-1, keepdims=True) + e_new + jnp.exp(sink - mx)
    o = jnp.einsum('bhk,bkc->bhc', e.astype(BF16), vc.astype(BF16), preferred_element_type=F32)
    o = o + e_new * vnew
    o = jnp.where(own[None], o * (1.0 / den), 0.0)
    o = o[:, :, 0:LANES] + o[:, :, LANES:2 * LANES]
    o = o + pltpu.roll(o, HEAD_DIM, axis=2)
    o_ref[...] = o[:, :, 0:HEAD_DIM]

    kout_ref[:, 0:WINDOW - 1, :] = kc_ref[:, 1:WINDOW, :]
    kout_ref[:, WINDOW - 1:WINDOW, :] = knew
    vout_ref[:, 0:WINDOW - 1, :] = vc_ref[:, 1:WINDOW, :]
    vout_ref[:, WINDOW - 1:WINDOW, :] = vnew


def _sample_attn(sinks, qx, knew, vnew, kc, vc):
    b = qx.shape[0]
    blk3 = lambda d1, d2: pl.BlockSpec((S2_BB, d1, d2), lambda i, *_: (i, 0, 0))
    return pl.pallas_call(
        _sample_attn_kernel,
        grid_spec=pltpu.PrefetchScalarGridSpec(
            num_scalar_prefetch=1, grid=(b // S2_BB,),
            in_specs=[blk3(N_HEADS, KV_DIM), blk3(1, KV_DIM), blk3(1, KV_DIM),
                      blk3(WINDOW, KV_DIM), blk3(WINDOW, KV_DIM)],
            out_specs=[blk3(N_HEADS, HEAD_DIM), blk3(WINDOW, KV_DIM), blk3(WINDOW, KV_DIM)]),
        out_shape=[jax.ShapeDtypeStruct((b, N_HEADS, HEAD_DIM), F32),
                   jax.ShapeDtypeStruct((b, WINDOW, KV_DIM), F32),
                   jax.ShapeDtypeStruct((b, WINDOW, KV_DIM), F32)],
        compiler_params=pltpu.CompilerParams(
            dimension_semantics=("parallel",), vmem_limit_bytes=V7X_VMEM_LIMIT),
        name="sample_attn",
    )(sinks, qx, knew, vnew, kc, vc)


def _sample_out_kernel(x_ref, o_ref, mab_ref, gc_ref, gpost_ref, gpre2_ref, gpost2_ref,
                       wc_ref, wo_ref, wg_ref, wu_ref, wd_ref, y_ref):
    y_c = _dot(o_ref[...].astype(BF16), wc_ref[...])
    merged = mab_ref[...] + gc_ref[...] * y_c
    x1 = x_ref[...] + _rmsnorm(_dot(merged.astype(BF16), wo_ref[...]), gpost_ref[...])
    y_ref[...] = _swiglu(x1, gpre2_ref[...], gpost2_ref[...], wg_ref, wu_ref, wd_ref)


def _sample_out(x, o, mab, gc, gpost, gpre2, gpost2, wc, wo, wg, wu, wd):
    in_arrays = (x, o, mab, gc, gpost, gpre2, gpost2, wc, wo, wg, wu, wd)
    return pl.pallas_call(
        _sample_out_kernel,
        grid=(1,),
        in_specs=[_resident(a.shape) for a in in_arrays],
        out_specs=pl.BlockSpec(x.shape, lambda i: (0, 0)),
        out_shape=jax.ShapeDtypeStruct(x.shape, F32),
        compiler_params=pltpu.CompilerParams(
            dimension_semantics=("arbitrary",), vmem_limit_bytes=V7X_VMEM_LIMIT),
        name="sample_out",
    )(*in_arrays)


def _rope_tables(pos):
    half = HEAD_DIM // 2
    inv = jnp.power(jnp.float32(ROPE_THETA), -jnp.arange(half, dtype=F32) * (2.0 / HEAD_DIM))
    ang = pos.astype(F32)[:, None] * inv[None, :]
    cos = jnp.cos(ang)
    sin = jnp.sin(ang)
    reps = LANES // HEAD_DIM
    cos_t = jnp.tile(jnp.concatenate([cos, cos], axis=1), (1, reps))
    sin_s = jnp.tile(jnp.concatenate([-sin, sin], axis=1), (1, reps))
    return cos_t, sin_s


def kernel(x_prompt, x_sample, state_conv, cache_win_k, cache_win_v, norm_pre_mix, norm_post_mix,
           norm_pre_ffn, norm_post_ffn, w_in, chunk_ln_g, chunk_ln_b, w_spatial, b_spatial, conv_w,
           conv_b, attn_sinks, w_br_a, w_br_b, w_br_c, w_out, w_ffn_gate, w_ffn_up, w_ffn_down):
    xp = x_prompt.reshape(SEQ, D_MODEL)
    xs = x_sample.reshape(DEC_BATCH, D_MODEL)
    cos_p, sin_p = _rope_tables(jnp.arange(SEQ, dtype=jnp.int32))
    cos_s, sin_s = _rope_tables(jnp.full((1,), PAST_LEN, dtype=jnp.int32))

    row = lambda a: a.reshape(1, -1)
    p_conv, p_k, p_v = [], [], []
    s_conv, s_k, s_v, s_cv = [], [], [], []
    for l in range(DEPTH):
        w_in_b = w_in[l].astype(BF16)
        wa, wb, wc, wo = (w_br_a[l].astype(BF16), w_br_b[l].astype(BF16),
                          w_br_c[l].astype(BF16), w_out[l].astype(BF16))
        wg, wu, wd = (w_ffn_gate[l].astype(BF16), w_ffn_up[l].astype(BF16),
                      w_ffn_down[l].astype(BF16))
        gpre, gpost = row(norm_pre_mix[l]), row(norm_post_mix[l])
        gpre2, gpost2 = row(norm_pre_ffn[l]), row(norm_post_ffn[l])
        lng, lnb = row(chunk_ln_g[l]), row(chunk_ln_b[l])
        convb = row(conv_b[l])
        bsp = jnp.repeat(jnp.transpose(b_spatial[l]), A_GROUP_DIM, axis=1)
        wsp0 = row(jnp.repeat(w_spatial[l][:, 0, 0], A_GROUP_DIM))
        bsp0 = row(jnp.repeat(b_spatial[l][:, 0], A_GROUP_DIM))

        x1, cxlast, klast, vlast = _prompt_mixer(
            xp, cos_p, sin_p, attn_sinks[l], gpre, gpost, w_in_b, lng, lnb, w_spatial[l], bsp,
            conv_w[l], convb, wa, wb, wc, wo)
        xp = _prompt_ffn(x1, gpre2, gpost2, wg, wu, wd)
        p_conv.append(cxlast[8 - (CONV_W - 1):].reshape(1, CONV_W - 1, D_MODEL))
        p_k.append(klast.reshape(1, WINDOW, N_KV_HEADS, HEAD_DIM))
        p_v.append(vlast.reshape(1, WINDOW, N_KV_HEADS, HEAD_DIM))

        hist = state_conv[l].reshape(DEC_BATCH, (CONV_W - 1) * D_MODEL)
        va, newconv, q, knew, vnew, mab, gc = _sample_proj(
            xs, hist, cos_s, sin_s, gpre, w_in_b, lng, lnb, wsp0, bsp0, conv_w[l], convb, wa, wb)
        qx = jnp.tile(q.reshape(DEC_BATCH, N_HEADS, HEAD_DIM), (1, 1, N_KV_HEADS))
        o, knext, vnext = _sample_attn(
            attn_sinks[l], qx, knew.reshape(DEC_BATCH, 1, KV_DIM), vnew.reshape(DEC_BATCH, 1, KV_DIM),
            cache_win_k[l].reshape(DEC_BATCH, WINDOW, KV_DIM),
            cache_win_v[l].reshape(DEC_BATCH, WINDOW, KV_DIM))
        xs = _sample_out(xs, o.reshape(DEC_BATCH, D_MODEL), mab, gc, gpost, gpre2, gpost2,
                         wc, wo, wg, wu, wd)
        s_conv.append(newconv.reshape(DEC_BATCH, CONV_W - 1, D_MODEL))
        s_k.append(knext.reshape(DEC_BATCH, WINDOW, N_KV_HEADS, HEAD_DIM))
        s_v.append(vnext.reshape(DEC_BATCH, WINDOW, N_KV_HEADS, HEAD_DIM))
        s_cv.append(va.reshape(DEC_BATCH, 1, D_MODEL))

    return (xp.reshape(1, SEQ, D_MODEL), xs.reshape(DEC_BATCH, 1, D_MODEL),
            jnp.stack(p_conv), jnp.stack(p_k), jnp.stack(p_v),
            jnp.stack(s_conv), jnp.stack(s_k), jnp.stack(s_v), jnp.stack(s_cv))
```

```python
import functools

import jax
import jax.numpy as jnp
from jax.experimental import pallas as pl
from jax.experimental.pallas import tpu as pltpu

D_MODEL = 1024
SEQ = 16384
DEPTH = 2
DEC_BATCH = 128
PAST_LEN = 16384
CHUNK = 128
A_GROUPS = 8
A_GROUP_DIM = D_MODEL // A_GROUPS
CONV_W = 3
N_HEADS = 16
N_KV_HEADS = 4
HEAD_DIM = 64
Q_PER_KV = N_HEADS // N_KV_HEADS
WINDOW = 128
ROPE_THETA = 10000.0
D_FF = 2816
KV_DIM = N_KV_HEADS * HEAD_DIM
KV_ROWS = WINDOW * N_KV_HEADS
NEG_INF = -1e30

OFF_UA = 0
OFF_VA = OFF_UA + D_MODEL
OFF_BG = OFF_VA + D_MODEL
OFF_CG = OFF_BG + D_MODEL
OFF_HB = OFF_CG + D_MODEL
OFF_Q = OFF_HB + D_MODEL
OFF_K = OFF_Q + N_HEADS * HEAD_DIM
OFF_V = OFF_K + KV_DIM
OFF_GA = OFF_V + KV_DIM
OFF_GB = OFF_GA + D_MODEL
OFF_GC = OFF_GB + D_MODEL
IN_WIDTH = OFF_GC + D_MODEL

LANES = 128
SUBLANES = 8
V7X_VMEM_LIMIT = 56 * 1024 * 1024

TM_MIX = 256
TM_FFN = 512
BLK = WINDOW
S2_BB = 8
ATTN_LOOKAHEAD = 2

assert CHUNK == BLK and WINDOW == BLK and PAST_LEN >= WINDOW
assert 2 * HEAD_DIM == LANES and A_GROUP_DIM == LANES
assert SEQ % TM_MIX == 0 and SEQ % TM_FFN == 0 and TM_MIX % BLK == 0 and DEC_BATCH % S2_BB == 0

BF16 = jnp.bfloat16
F32 = jnp.float32


def _dot(a, b):
    return jnp.dot(a, b, preferred_element_type=F32)


def _dot_nt(a, b):
    return jax.lax.dot_general(a, b, (((1,), (1,)), ((), ())), preferred_element_type=F32)


def _rmsnorm(x, g, eps=1e-6):
    return x * jax.lax.rsqrt(jnp.mean(x * x, axis=-1, keepdims=True) + eps) * g


def _layernorm(x, g, b, eps=1e-5):
    mu = jnp.mean(x, axis=-1, keepdims=True)
    xc = x - mu
    var = jnp.mean(xc * xc, axis=-1, keepdims=True)
    return xc * jax.lax.rsqrt(var + eps) * g + b


def _lane_iota(shape):
    return jax.lax.broadcasted_iota(jnp.int32, shape, len(shape) - 1)


def _rope_block(xb, cos_t, sin_s):
    lane = _lane_iota(xb.shape)
    first_half = (lane % HEAD_DIM) < (HEAD_DIM // 2)
    rot = jnp.where(first_half,
                    pltpu.roll(xb, LANES - HEAD_DIM // 2, axis=1),
                    pltpu.roll(xb, HEAD_DIM // 2, axis=1))
    return xb * cos_t + rot * sin_s


def _rope(x, cos_t, sin_s):
    n = x.shape[1] // LANES
    return jnp.concatenate(
        [_rope_block(x[:, j * LANES:(j + 1) * LANES], cos_t, sin_s) for j in range(n)], axis=1)


def _split_kv_heads(blk):
    out = []
    for j in range(KV_DIM // LANES):
        b = blk[:, j * LANES:(j + 1) * LANES]
        r = pltpu.roll(b, HEAD_DIM, axis=1)
        low = _lane_iota(b.shape) < HEAD_DIM
        zero = jnp.zeros_like(b)
        out.append((jnp.where(low, b, zero).astype(BF16), jnp.where(low, zero, r).astype(BF16)))
        out.append((jnp.where(low, r, zero).astype(BF16), jnp.where(low, zero, b).astype(BF16)))
    return out


def _layer_block(layer, shape, single_buffer=True):
    nd = len(shape)
    mode = pl.Buffered(1) if single_buffer else None
    return pl.BlockSpec((None,) + tuple(shape), lambda *_: (layer,) + (0,) * nd, pipeline_mode=mode)


def _whole(shape):
    nd = len(shape)
    return pl.BlockSpec(tuple(shape), lambda *_: (0,) * nd, pipeline_mode=pl.Buffered(1))


def _mixer_kernel(layer, sinks_ref, x_ref, cos_a_ref, sin_a_ref, cos_b_ref, sin_b_ref,
                  cos_bs_ref, sin_bs_ref, gpre_ref, gpost_ref, w_in_ref,
                  lng_ref, lnb_ref, wsp_ref, bsp_ref, convw_ref, convb_ref,
                  wa_ref, wb_ref, wc_ref, wo_ref,
                  x1_ref, cxlast_ref, klast_ref, vlast_ref,
                  wsp_s, kprev_s, vprev_s, cxprev_s):
    step = pl.program_id(0)
    nblk = TM_MIX // BLK

    @pl.when(step == 0)
    def _init():
        row = jax.lax.broadcasted_iota(jnp.int32, (CHUNK, CHUNK), 0)
        col = jax.lax.broadcasted_iota(jnp.int32, (CHUNK, CHUNK), 1)
        for g in range(A_GROUPS):
            wsp_s[g] = jnp.where(col <= row, wsp_ref[g], 0.0).astype(BF16)
        kprev_s[...] = jnp.zeros_like(kprev_s)
        vprev_s[...] = jnp.zeros_like(vprev_s)
        cxprev_s[...] = jnp.zeros_like(cxprev_s)

    x = x_ref[...]
    h = _rmsnorm(x, gpre_ref[...]).astype(BF16)

    def proj(off, width):
        return _dot(h, w_in_ref[:, off:off + width])


    va = _layernorm(jax.nn.gelu(proj(OFF_VA, D_MODEL)), lng_ref[...], lnb_ref[...]).astype(BF16)
    ua = jax.nn.gelu(proj(OFF_UA, D_MODEL))
    cx = proj(OFF_CG, D_MODEL) * proj(OFF_HB, D_MODEL)
    prev = cxprev_s[...]
    rowi = jax.lax.broadcasted_iota(jnp.int32, cx.shape, 0)
    cx1 = jnp.where(rowi < 1, prev[SUBLANES - 1:SUBLANES, :], pltpu.roll(cx, 1, axis=0))
    cx2 = jnp.where(rowi < 2,
                    jnp.where(rowi < 1, prev[SUBLANES - 2:SUBLANES - 1, :],
                              prev[SUBLANES - 1:SUBLANES, :]),
                    pltpu.roll(cx, 2, axis=0))
    conv = convb_ref[...] + convw_ref[0:1, :] * cx2
    conv = conv + convw_ref[1:2, :] * cx1
    conv = conv + convw_ref[2:3, :] * cx
    cxprev_s[...] = cx[TM_MIX - SUBLANES:, :]
    cxlast_ref[...] = cx[TM_MIX - SUBLANES:, :]
    bx = (proj(OFF_BG, D_MODEL) * conv).astype(BF16)
    sa_rows = []
    for c in range(nblk):
        sa_rows.append(jnp.concatenate(
            [_dot(wsp_s[g], va[c * BLK:(c + 1) * BLK, g * LANES:(g + 1) * LANES])
             for g in range(A_GROUPS)], axis=1) + bsp_ref[...])
    ax = (ua * jnp.concatenate(sa_rows, axis=0)).astype(BF16)

    cos_a, sin_a = cos_a_ref[...], sin_a_ref[...]
    cos_t = cos_a * cos_b_ref[...] - sin_a * sin_b_ref[...]
    sin_s = sin_a * cos_bs_ref[...] + cos_a * sin_bs_ref[...]
    q = (_rope(proj(OFF_Q, N_HEADS * HEAD_DIM), cos_t, sin_s) * (HEAD_DIM ** -0.5)).astype(BF16)
    k = _rope(proj(OFF_K, KV_DIM), cos_t, sin_s)
    v = proj(OFF_V, KV_DIM)
    klast_ref[...] = k[TM_MIX - BLK:, :]
    vlast_ref[...] = v[TM_MIX - BLK:, :]

    y_a = _dot(ax, wa_ref[...])
    y_b = _dot(bx, wb_ref[...])
    merged = jax.nn.sigmoid(proj(OFF_GA, D_MODEL)) * y_a
    merged = merged + jax.nn.sigmoid(proj(OFF_GB, D_MODEL)) * y_b
    gate_c = jax.nn.sigmoid(proj(OFF_GC, D_MODEL))

    qi = jax.lax.broadcasted_iota(jnp.int32, (BLK, 2 * BLK), 0)
    kj = jax.lax.broadcasted_iota(jnp.int32, (BLK, 2 * BLK), 1)
    band = (kj >= qi) & (kj <= qi + WINDOW)
    lane_low = _lane_iota((BLK, LANES)) < HEAD_DIM

    k_parts = [[(kprev_s[g, 0], kprev_s[g, 1]) for g in range(N_KV_HEADS)]]
    v_parts = [[(vprev_s[g, 0], vprev_s[g, 1]) for g in range(N_KV_HEADS)]]
    for c in range(nblk):
        k_parts.append(_split_kv_heads(k[c * BLK:(c + 1) * BLK, :]))
        v_parts.append(_split_kv_heads(v[c * BLK:(c + 1) * BLK, :]))
    for g in range(N_KV_HEADS):
        kprev_s[g, 0] = k_parts[nblk][g][0]
        kprev_s[g, 1] = k_parts[nblk][g][1]
        vprev_s[g, 0] = v_parts[nblk][g][0]
        vprev_s[g, 1] = v_parts[nblk][g][1]
    mask_first = band & (kj >= jnp.where(step == 0, BLK, 0))

    def block_diag(parts, c, g):
        return jnp.concatenate([parts[c][g][0], parts[c + 1][g][0],
                                parts[c][g][1], parts[c + 1][g][1]], axis=0)

    def scores(c, m):
        kbd = block_diag(k_parts, c, (2 * m) // Q_PER_KV)
        return _dot_nt(q[c * BLK:(c + 1) * BLK, m * LANES:(m + 1) * LANES], kbd)

    def softmax_numerators(c, m, s):
        mask = mask_first if c == 0 else band
        es, invs = [], []
        for hh in range(2):
            sink = sinks_ref[layer, 2 * m + hh]
            sh = jnp.where(mask, s[:, hh * 2 * BLK:(hh + 1) * 2 * BLK], NEG_INF)
            mx = jnp.maximum(jnp.max(sh, axis=-1, keepdims=True), sink)
            e = jnp.exp(sh - mx)
            den = jnp.sum(e, axis=-1, keepdims=True) + jnp.exp(sink - mx)
            es.append(e.astype(BF16))
            invs.append(1.0 / den)
        return jnp.concatenate(es, axis=1), jnp.where(lane_low, invs[0], invs[1])

    def weighted_values(c, m, e, inv):
        vbd = block_diag(v_parts, c, (2 * m) // Q_PER_KV)
        return _dot(e, vbd) * inv

    its = [(c, m) for c in range(nblk) for m in range(N_HEADS // 2)]
    s_queue = [scores(*its[i]) for i in range(min(ATTN_LOOKAHEAD, len(its)))]
    o_parts = {}
    for i, (c, m) in enumerate(its):
        if i + ATTN_LOOKAHEAD < len(its):
            s_queue.append(scores(*its[i + ATTN_LOOKAHEAD]))
        e, inv = softmax_numerators(c, m, s_queue.pop(0))
        o_parts[(c, m)] = weighted_values(c, m, e, inv)
    o = jnp.concatenate(
        [jnp.concatenate([o_parts[(c, m)] for m in range(N_HEADS // 2)], axis=1)
         for c in range(nblk)], axis=0).astype(BF16)
    y_c = _dot(o, wc_ref[...])
    merged = merged + gate_c * y_c

    x1_ref[...] = x + _rmsnorm(_dot(merged.astype(BF16), wo_ref[...]), gpost_ref[...])


def _prompt_mixer(layer, x, rope, sinks, gpre, gpost, w_in, lng, lnb, wsp, bsp, convw, convb,
                  wa, wb, wc, wo):
    n = x.shape[0]
    grid = (n // TM_MIX,)
    lb = functools.partial(_layer_block, layer)
    in_specs = [
        pl.BlockSpec((TM_MIX, D_MODEL), lambda i, *_: (i, 0)),
        pl.BlockSpec((None, 1, LANES), lambda i, *_: (i, 0, 0)),
        pl.BlockSpec((None, 1, LANES), lambda i, *_: (i, 0, 0)),
        _whole((TM_MIX, LANES)), _whole((TM_MIX, LANES)),
        _whole((TM_MIX, LANES)), _whole((TM_MIX, LANES)),
        lb((1, D_MODEL)), lb((1, D_MODEL)),
        lb((D_MODEL, IN_WIDTH)),
        lb((1, D_MODEL)), lb((1, D_MODEL)),
        lb((A_GROUPS, CHUNK, CHUNK)), lb((CHUNK, D_MODEL)),
        lb((CONV_W, D_MODEL)), lb((1, D_MODEL)),
        lb((D_MODEL, D_MODEL)), lb((D_MODEL, D_MODEL)),
        lb((D_MODEL, D_MODEL)), lb((D_MODEL, D_MODEL)),
    ]
    out_specs = [
        pl.BlockSpec((TM_MIX, D_MODEL), lambda i, *_: (i, 0)),
        pl.BlockSpec((SUBLANES, D_MODEL), lambda i, *_: (0, 0)),
        pl.BlockSpec((BLK, KV_DIM), lambda i, *_: (0, 0)),
        pl.BlockSpec((BLK, KV_DIM), lambda i, *_: (0, 0)),
    ]
    out_shape = [
        jax.ShapeDtypeStruct((n, D_MODEL), F32),
        jax.ShapeDtypeStruct((SUBLANES, D_MODEL), F32),
        jax.ShapeDtypeStruct((BLK, KV_DIM), F32),
        jax.ShapeDtypeStruct((BLK, KV_DIM), F32),
    ]
    scratch = [
        pltpu.VMEM((A_GROUPS, CHUNK, CHUNK), BF16),
        pltpu.VMEM((N_KV_HEADS, 2, BLK, LANES), BF16),
        pltpu.VMEM((N_KV_HEADS, 2, BLK, LANES), BF16),
        pltpu.VMEM((SUBLANES, D_MODEL), F32),
    ]
    return pl.pallas_call(
        functools.partial(_mixer_kernel, layer),
        grid_spec=pltpu.PrefetchScalarGridSpec(
            num_scalar_prefetch=1, grid=grid, in_specs=in_specs, out_specs=out_specs,
            scratch_shapes=scratch),
        out_shape=out_shape,
        compiler_params=pltpu.CompilerParams(
            dimension_semantics=("arbitrary",), vmem_limit_bytes=V7X_VMEM_LIMIT),
        name="prompt_mixer",
    )(sinks, x, *rope, gpre, gpost, w_in, lng, lnb, wsp, bsp, convw, convb, wa, wb, wc, wo)


def _swiglu(x, gpre, gpost, wg_ref, wu_ref, wd_ref):
    h = _rmsnorm(x, gpre).astype(BF16)
    a = jax.nn.silu(_dot(h, wg_ref[...])) * _dot(h, wu_ref[...])
    return x + _rmsnorm(_dot(a.astype(BF16), wd_ref[...]), gpost)


def _ffn_kernel(x_ref, gpre_ref, gpost_ref, wg_ref, wu_ref, wd_ref, y_ref):
    y_ref[...] = _swiglu(x_ref[...], gpre_ref[...], gpost_ref[...], wg_ref, wu_ref, wd_ref)


def _prompt_ffn(layer, x, gpre, gpost, wg, wu, wd):
    n = x.shape[0]
    lb = functools.partial(_layer_block, layer)
    row_tile = pl.BlockSpec((TM_FFN, D_MODEL), lambda i: (i, 0))
    return pl.pallas_call(
        _ffn_kernel,
        grid=(n // TM_FFN,),
        in_specs=[row_tile, lb((1, D_MODEL)), lb((1, D_MODEL)),
                  lb((D_MODEL, D_FF)), lb((D_MODEL, D_FF)), lb((D_FF, D_MODEL))],
        out_specs=row_tile,
        out_shape=jax.ShapeDtypeStruct((n, D_MODEL), F32),
        compiler_params=pltpu.CompilerParams(
            dimension_semantics=("parallel",), vmem_limit_bytes=V7X_VMEM_LIMIT),
        name="prompt_ffn",
    )(x, gpre, gpost, wg, wu, wd)


def _sample_proj_kernel(x_ref, hist_ref, cos_ref, sin_ref, gpre_ref, w_in_ref, lng_ref, lnb_ref,
                        wsp0_ref, bsp0_ref, convw_ref, convb_ref, wa_ref, wb_ref,
                        va_ref, newconv_ref, q_ref, k_ref, v_ref, mab_ref, gc_ref):
    x = x_ref[...]
    h = _rmsnorm(x, gpre_ref[...]).astype(BF16)

    def proj(off, width):
        return _dot(h, w_in_ref[:, off:off + width])

    ua = jax.nn.gelu(proj(OFF_UA, D_MODEL))
    va = _layernorm(jax.nn.gelu(proj(OFF_VA, D_MODEL)), lng_ref[...], lnb_ref[...])
    va_ref[...] = va
    sa = wsp0_ref[...] * va + bsp0_ref[...]
    y_a = _dot((ua * sa).astype(BF16), wa_ref[...])
    merged = jax.nn.sigmoid(proj(OFF_GA, D_MODEL)) * y_a

    cx = proj(OFF_CG, D_MODEL) * proj(OFF_HB, D_MODEL)
    h0 = hist_ref[:, 0:D_MODEL]
    h1 = hist_ref[:, D_MODEL:2 * D_MODEL]
    conv = convb_ref[...] + convw_ref[0:1, :] * h0
    conv = conv + convw_ref[1:2, :] * h1
    conv = conv + convw_ref[2:3, :] * cx
    newconv_ref[:, 0:D_MODEL] = h1
    newconv_ref[:, D_MODEL:2 * D_MODEL] = cx
    y_b = _dot((proj(OFF_BG, D_MODEL) * conv).astype(BF16), wb_ref[...])
    mab_ref[...] = merged + jax.nn.sigmoid(proj(OFF_GB, D_MODEL)) * y_b
    gc_ref[...] = jax.nn.sigmoid(proj(OFF_GC, D_MODEL))

    cos_t = cos_ref[...]
    sin_s = sin_ref[...]
    q_ref[...] = _rope(proj(OFF_Q, N_HEADS * HEAD_DIM), cos_t, sin_s) * (HEAD_DIM ** -0.5)
    k_ref[...] = _rope(proj(OFF_K, KV_DIM), cos_t, sin_s)
    v_ref[...] = proj(OFF_V, KV_DIM)


def _sample_proj(layer, x, hist, cos_t, sin_s, gpre, w_in, lng, lnb, wsp0, bsp0, convw, convb,
                 wa, wb):
    b = x.shape[0]
    lb = functools.partial(_layer_block, layer)
    in_specs = [
        _whole(x.shape), lb((b, (CONV_W - 1) * D_MODEL)), _whole(cos_t.shape), _whole(sin_s.shape),
        lb((1, D_MODEL)), lb((D_MODEL, IN_WIDTH)), lb((1, D_MODEL)), lb((1, D_MODEL)),
        lb((1, D_MODEL)), lb((1, D_MODEL)), lb((CONV_W, D_MODEL)), lb((1, D_MODEL)),
        lb((D_MODEL, D_MODEL)), lb((D_MODEL, D_MODEL)),
    ]
    out_shape = [
        jax.ShapeDtypeStruct((b, D_MODEL), F32),
        jax.ShapeDtypeStruct((b, 2 * D_MODEL), F32),
        jax.ShapeDtypeStruct((b, D_MODEL), F32),
        jax.ShapeDtypeStruct((b, KV_DIM), F32),
        jax.ShapeDtypeStruct((b, KV_DIM), F32),
        jax.ShapeDtypeStruct((b, D_MODEL), F32),
        jax.ShapeDtypeStruct((b, D_MODEL), F32),
    ]
    return pl.pallas_call(
        _sample_proj_kernel,
        grid=(1,),
        in_specs=in_specs,
        out_specs=[pl.BlockSpec(s.shape, lambda i: (0, 0)) for s in out_shape],
        out_shape=out_shape,
        compiler_params=pltpu.CompilerParams(
            dimension_semantics=("arbitrary",), vmem_limit_bytes=V7X_VMEM_LIMIT),
        name="sample_proj",
    )(x, hist, cos_t, sin_s, gpre, w_in, lng, lnb, wsp0, bsp0, convw, convb, wa, wb)


def _sample_attn_kernel(layer, n_aliased, sinks_ref, *refs):
    (q_ref, knew_ref, vnew_ref, knx_ref, vnx_ref, kc_ref, vc_ref,
     o_ref, kout_ref, vout_ref) = refs[n_aliased:]
    qf = q_ref[...]
    qb = qf.astype(BF16)
    s = jnp.einsum('bhd,bkd->bhk', qb, kc_ref[...].astype(BF16), preferred_element_type=F32)
    head = jax.lax.broadcasted_iota(jnp.int32, (1, N_HEADS, KV_ROWS), 1)
    crow = jax.lax.broadcasted_iota(jnp.int32, (1, N_HEADS, KV_ROWS), 2)
    own = (crow % N_KV_HEADS) == (head // Q_PER_KV)
    s = jnp.where(own, s, NEG_INF)
    s_new = jnp.sum(qf * knx_ref[...], axis=-1, keepdims=True)
    hrow = jax.lax.broadcasted_iota(jnp.int32, (1, N_HEADS, 1), 1)
    sink = jnp.zeros((1, N_HEADS, 1), F32)
    for hh in range(N_HEADS):
        sink = jnp.where(hrow == hh, sinks_ref[layer, hh], sink)
    mx = jnp.maximum(jnp.maximum(jnp.max(s, axis=-1, keepdims=True), s_new), sink)
    e = jnp.exp(s - mx)
    e_new = jnp.exp(s_new - mx)
    den = jnp.sum(e, axis=-1, keepdims=True) + e_new + jnp.exp(sink - mx)
    o = jnp.einsum('bhk,bkd->bhd', e.astype(BF16), vc_ref[...].astype(BF16),
                   preferred_element_type=F32)
    o_ref[...] = (o + e_new * vnx_ref[...]) * (1.0 / den)

    kout_ref[:, 0:KV_ROWS - N_KV_HEADS, :] = kc_ref[:, N_KV_HEADS:KV_ROWS, :]
    kout_ref[:, KV_ROWS - N_KV_HEADS:KV_ROWS, :] = knew_ref[...]
    vout_ref[:, 0:KV_ROWS - N_KV_HEADS, :] = vc_ref[:, N_KV_HEADS:KV_ROWS, :]
    vout_ref[:, KV_ROWS - N_KV_HEADS:KV_ROWS, :] = vnew_ref[...]


def _sample_attn(layer, sinks, q, knew, vnew, knx, vnx, kcache, vcache, kout_prev, vout_prev):
    b = q.shape[0]
    batch3 = lambda d1, d2: pl.BlockSpec((S2_BB, d1, d2), lambda i, *_: (i, 0, 0))
    cache_blk = pl.BlockSpec((None, S2_BB, KV_ROWS, HEAD_DIM), lambda i, *_: (layer, i, 0, 0))
    aliased = [] if kout_prev is None else [kout_prev, vout_prev]
    n_al = len(aliased)
    in_specs = ([pl.BlockSpec(memory_space=pl.ANY)] * n_al +
                [batch3(N_HEADS, HEAD_DIM), batch3(N_KV_HEADS, HEAD_DIM),
                 batch3(N_KV_HEADS, HEAD_DIM), batch3(N_HEADS, HEAD_DIM),
                 batch3(N_HEADS, HEAD_DIM), cache_blk, cache_blk])
    io_alias = {1: 1, 2: 2} if n_al else {}
    return pl.pallas_call(
        functools.partial(_sample_attn_kernel, layer, n_al),
        grid_spec=pltpu.PrefetchScalarGridSpec(
            num_scalar_prefetch=1, grid=(b // S2_BB,),
            in_specs=in_specs,
            out_specs=[batch3(N_HEADS, HEAD_DIM), cache_blk, cache_blk]),
        out_shape=[jax.ShapeDtypeStruct((b, N_HEADS, HEAD_DIM), F32),
                   jax.ShapeDtypeStruct(kcache.shape, F32),
                   jax.ShapeDtypeStruct(vcache.shape, F32)],
        input_output_aliases=io_alias,
        compiler_params=pltpu.CompilerParams(
            dimension_semantics=("parallel",), vmem_limit_bytes=V7X_VMEM_LIMIT),
        name="sample_attn",
    )(sinks, *aliased, q, knew, vnew, knx, vnx, kcache, vcache)


def _sample_out_kernel(x_ref, o_ref, mab_ref, gc_ref, gpost_ref, gpre2_ref, gpost2_ref,
                       wc_ref, wo_ref, wg_ref, wu_ref, wd_ref, y_ref):
    y_c = _dot(o_ref[...].astype(BF16), wc_ref[...])
    merged = mab_ref[...] + gc_ref[...] * y_c
    x1 = x_ref[...] + _rmsnorm(_dot(merged.astype(BF16), wo_ref[...]), gpost_ref[...])
    y_ref[...] = _swiglu(x1, gpre2_ref[...], gpost2_ref[...], wg_ref, wu_ref, wd_ref)


def _sample_out(layer, x, o, mab, gc, gpost, gpre2, gpost2, wc, wo, wg, wu, wd):
    lb = functools.partial(_layer_block, layer)
    in_specs = [_whole(x.shape), _whole(o.shape), _whole(mab.shape), _whole(gc.shape),
                lb((1, D_MODEL)), lb((1, D_MODEL)), lb((1, D_MODEL)),
                lb((D_MODEL, D_MODEL)), lb((D_MODEL, D_MODEL)),
                lb((D_MODEL, D_FF)), lb((D_MODEL, D_FF)), lb((D_FF, D_MODEL))]
    return pl.pallas_call(
        _sample_out_kernel,
        grid=(1,),
        in_specs=in_specs,
        out_specs=pl.BlockSpec(x.shape, lambda i: (0, 0)),
        out_shape=jax.ShapeDtypeStruct(x.shape, F32),
        compiler_params=pltpu.CompilerParams(
            dimension_semantics=("arbitrary",), vmem_limit_bytes=V7X_VMEM_LIMIT),
        name="sample_out",
    )(x, o, mab, gc, gpost, gpre2, gpost2, wc, wo, wg, wu, wd)


def _rope_angles(pos):
    half = HEAD_DIM // 2
    inv = jnp.power(jnp.float32(ROPE_THETA), -jnp.arange(half, dtype=F32) * (2.0 / HEAD_DIM))
    return pos.astype(F32)[:, None] * jnp.tile(inv, LANES // half)[None, :]


def _rope_sign():
    lane = jnp.arange(LANES, dtype=jnp.int32)
    return jnp.where((lane % HEAD_DIM) < (HEAD_DIM // 2), -1.0, 1.0).astype(F32)[None, :]


def kernel(x_prompt, x_sample, state_conv, cache_win_k, cache_win_v, norm_pre_mix, norm_post_mix,
           norm_pre_ffn, norm_post_ffn, w_in, chunk_ln_g, chunk_ln_b, w_spatial, b_spatial, conv_w,
           conv_b, attn_sinks, w_br_a, w_br_b, w_br_c, w_out, w_ffn_gate, w_ffn_up, w_ffn_down):
    xp = x_prompt.reshape(SEQ, D_MODEL)
    xs = x_sample.reshape(DEC_BATCH, D_MODEL)

    sign = _rope_sign()
    ang_a = _rope_angles(jnp.arange(SEQ // TM_MIX, dtype=jnp.int32) * TM_MIX)
    ang_b = _rope_angles(jnp.arange(TM_MIX, dtype=jnp.int32))
    rope_p = (jnp.cos(ang_a)[:, None, :], jnp.sin(ang_a)[:, None, :],
              jnp.cos(ang_b), jnp.sin(ang_b), sign * jnp.cos(ang_b), sign * jnp.sin(ang_b))
    ang_s = _rope_angles(jnp.full((1,), PAST_LEN, dtype=jnp.int32))
    cos_s, sin_s = jnp.cos(ang_s), sign * jnp.sin(ang_s)

    w_in_b = w_in.astype(BF16)
    wa, wb, wc, wo = (w_br_a.astype(BF16), w_br_b.astype(BF16), w_br_c.astype(BF16),
                      w_out.astype(BF16))
    wg, wu, wd = w_ffn_gate.astype(BF16), w_ffn_up.astype(BF16), w_ffn_down.astype(BF16)
    rows = lambda a: a.reshape(DEPTH, 1, -1)
    gpre, gpost = rows(norm_pre_mix), rows(norm_post_mix)
    gpre2, gpost2 = rows(norm_pre_ffn), rows(norm_post_ffn)
    lng, lnb, convb = rows(chunk_ln_g), rows(chunk_ln_b), rows(conv_b)
    bsp = jnp.repeat(jnp.transpose(b_spatial, (0, 2, 1)), A_GROUP_DIM, axis=2)
    wsp0 = rows(jnp.repeat(w_spatial[:, :, 0, 0], A_GROUP_DIM, axis=1))
    bsp0 = rows(jnp.repeat(b_spatial[:, :, 0], A_GROUP_DIM, axis=1))
    hist = state_conv.reshape(DEPTH, DEC_BATCH, (CONV_W - 1) * D_MODEL)
    kcache = cache_win_k.reshape(DEPTH, DEC_BATCH, KV_ROWS, HEAD_DIM)
    vcache = cache_win_v.reshape(DEPTH, DEC_BATCH, KV_ROWS, HEAD_DIM)

    p_conv, p_k, p_v = [], [], []
    s_conv, s_cv = [], []
    knext = vnext = None
    for l in range(DEPTH):
        x1, cxlast, klast, vlast = _prompt_mixer(
            l, xp, rope_p, attn_sinks, gpre, gpost, w_in_b, lng, lnb, w_spatial, bsp,
            conv_w, convb, wa, wb, wc, wo)
        xp = _prompt_ffn(l, x1, gpre2, gpost2, wg, wu, wd)
        p_conv.append(cxlast[SUBLANES - (CONV_W - 1):].reshape(1, CONV_W - 1, D_MODEL))
        p_k.append(klast.reshape(1, WINDOW, N_KV_HEADS, HEAD_DIM))
        p_v.append(vlast.reshape(1, WINDOW, N_KV_HEADS, HEAD_DIM))

        va, newconv, q, knew, vnew, mab, gc = _sample_proj(
            l, xs, hist, cos_s, sin_s, gpre, w_in_b, lng, lnb, wsp0, bsp0, conv_w, convb, wa, wb)
        knew = knew.reshape(DEC_BATCH, N_KV_HEADS, HEAD_DIM)
        vnew = vnew.reshape(DEC_BATCH, N_KV_HEADS, HEAD_DIM)
        o, knext, vnext = _sample_attn(
            l, attn_sinks, q.reshape(DEC_BATCH, N_HEADS, HEAD_DIM), knew, vnew,
            jnp.repeat(knew, Q_PER_KV, axis=1), jnp.repeat(vnew, Q_PER_KV, axis=1),
            kcache, vcache, knext, vnext)
        xs = _sample_out(l, xs, o.reshape(DEC_BATCH, D_MODEL), mab, gc, gpost, gpre2, gpost2,
                         wc, wo, wg, wu, wd)
        s_conv.append(newconv.reshape(DEC_BATCH, CONV_W - 1, D_MODEL))
        s_cv.append(va.reshape(DEC_BATCH, 1, D_MODEL))

    cache_shape = (DEPTH, DEC_BATCH, WINDOW, N_KV_HEADS, HEAD_DIM)
    return (xp.reshape(1, SEQ, D_MODEL), xs.reshape(DEC_BATCH, 1, D_MODEL),
            jnp.stack(p_conv), jnp.stack(p_k), jnp.stack(p_v),
            jnp.stack(s_conv), knext.reshape(cache_shape), vnext.reshape(cache_shape),
            jnp.stack(s_cv))
```

```python
import functools

import jax
import jax.numpy as jnp
from jax.experimental import pallas as pl
from jax.experimental.pallas import tpu as pltpu

D_MODEL = 1024
SEQ = 16384
DEPTH = 2
DEC_BATCH = 128
PAST_LEN = 16384
CHUNK = 128
A_GROUPS = 8
A_GROUP_DIM = D_MODEL // A_GROUPS
CONV_W = 3
N_HEADS = 16
N_KV_HEADS = 4
HEAD_DIM = 64
Q_PER_KV = N_HEADS // N_KV_HEADS
WINDOW = 128
ROPE_THETA = 10000.0
D_FF = 2816
KV_DIM = N_KV_HEADS * HEAD_DIM
NEG_INF = -1e30

OFF_UA = 0
OFF_VA = OFF_UA + D_MODEL
OFF_BG = OFF_VA + D_MODEL
OFF_CG = OFF_BG + D_MODEL
OFF_HB = OFF_CG + D_MODEL
OFF_Q = OFF_HB + D_MODEL
OFF_K = OFF_Q + N_HEADS * HEAD_DIM
OFF_V = OFF_K + KV_DIM
OFF_GA = OFF_V + KV_DIM
OFF_GB = OFF_GA + D_MODEL
OFF_GC = OFF_GB + D_MODEL
IN_WIDTH = OFF_GC + D_MODEL

LANES = 128
SUBLANES = 8
V7X_VMEM_LIMIT = 56 * 1024 * 1024

TM_MIX = 256
TM_FFN = 512
BLK = WINDOW
S2_GROUPS = 32
ATTN_LOOKAHEAD = 2
GATE_COLS = 256
PIECES_BEFORE_ATTN = 3

assert CHUNK == BLK and WINDOW == BLK and PAST_LEN >= WINDOW
assert 2 * HEAD_DIM == LANES and A_GROUP_DIM == LANES
assert SEQ % TM_MIX == 0 and SEQ % TM_FFN == 0 and TM_MIX % BLK == 0
assert LANES % S2_GROUPS == 0 and S2_GROUPS % N_KV_HEADS == 0 and WINDOW == LANES
assert (DEC_BATCH * N_KV_HEADS) % LANES == 0

BF16 = jnp.bfloat16
F32 = jnp.float32


def _dot(a, b):
    return jnp.dot(a, b, preferred_element_type=F32)


def _dot_nt(a, b):
    return jax.lax.dot_general(a, b, (((1,), (1,)), ((), ())), preferred_element_type=F32)


def _rmsnorm(x, g, eps=1e-6):
    return x * jax.lax.rsqrt(jnp.mean(x * x, axis=-1, keepdims=True) + eps) * g


def _layernorm(x, g, b, eps=1e-5):
    mu = jnp.mean(x, axis=-1, keepdims=True)
    xc = x - mu
    var = jnp.mean(xc * xc, axis=-1, keepdims=True)
    return xc * jax.lax.rsqrt(var + eps) * g + b


def _lane_iota(shape):
    return jax.lax.broadcasted_iota(jnp.int32, shape, len(shape) - 1)


def _rope_block(xb, cos_t, sin_s):
    lane = _lane_iota(xb.shape)
    first_half = (lane % HEAD_DIM) < (HEAD_DIM // 2)
    rot = jnp.where(first_half,
                    pltpu.roll(xb, LANES - HEAD_DIM // 2, axis=1),
                    pltpu.roll(xb, HEAD_DIM // 2, axis=1))
    return xb * cos_t + rot * sin_s


def _rope(x, cos_t, sin_s):
    n = x.shape[1] // LANES
    return jnp.concatenate(
        [_rope_block(x[:, j * LANES:(j + 1) * LANES], cos_t, sin_s) for j in range(n)], axis=1)


def _split_kv_heads(blk):
    out = []
    for j in range(KV_DIM // LANES):
        b = blk[:, j * LANES:(j + 1) * LANES]
        r = pltpu.roll(b, HEAD_DIM, axis=1)
        low = _lane_iota(b.shape) < HEAD_DIM
        zero = jnp.zeros_like(b)
        out.append((jnp.where(low, b, zero).astype(BF16), jnp.where(low, zero, r).astype(BF16)))
        out.append((jnp.where(low, r, zero).astype(BF16), jnp.where(low, zero, b).astype(BF16)))
    return out


def _layer_block(layer, shape, single_buffer=True):
    nd = len(shape)
    mode = pl.Buffered(1) if single_buffer else None
    return pl.BlockSpec((None,) + tuple(shape), lambda *_: (layer,) + (0,) * nd, pipeline_mode=mode)


def _whole(shape):
    nd = len(shape)
    return pl.BlockSpec(tuple(shape), lambda *_: (0,) * nd, pipeline_mode=pl.Buffered(1))


def _mixer_kernel(layer, sinks_ref, x_ref, cos_a_ref, sin_a_ref, cos_b_ref, sin_b_ref,
                  cos_bs_ref, sin_bs_ref, gpre_ref, gpost_ref, w_in_ref,
                  lng_ref, lnb_ref, wsp_ref, bsp_ref, convw_ref, convb_ref,
                  wa_ref, wb_ref, wc_ref, wo_ref,
                  x1_ref, cxlast_ref, klast_ref, vlast_ref,
                  wsp_s, kprev_s, vprev_s, cxprev_s):
    step = pl.program_id(0)
    nblk = TM_MIX // BLK

    @pl.when(step == 0)
    def _init():
        row = jax.lax.broadcasted_iota(jnp.int32, (CHUNK, CHUNK), 0)
        col = jax.lax.broadcasted_iota(jnp.int32, (CHUNK, CHUNK), 1)
        for g in range(A_GROUPS):
            wsp_s[g] = jnp.where(col <= row, wsp_ref[g], 0.0).astype(BF16)
        kprev_s[...] = jnp.zeros_like(kprev_s)
        vprev_s[...] = jnp.zeros_like(vprev_s)
        cxprev_s[...] = jnp.zeros_like(cxprev_s)

    x = x_ref[...]
    h = _rmsnorm(x, gpre_ref[...]).astype(BF16)

    def proj(off, width):
        return _dot(h, w_in_ref[:, off:off + width])

    va = _layernorm(jax.nn.gelu(proj(OFF_VA, D_MODEL)), lng_ref[...], lnb_ref[...]).astype(BF16)
    ua = jax.nn.gelu(proj(OFF_UA, D_MODEL))
    cx = proj(OFF_CG, D_MODEL) * proj(OFF_HB, D_MODEL)
    prev = cxprev_s[...]
    rowi = jax.lax.broadcasted_iota(jnp.int32, cx.shape, 0)
    cx1 = jnp.where(rowi < 1, prev[SUBLANES - 1:SUBLANES, :], pltpu.roll(cx, 1, axis=0))
    cx2 = jnp.where(rowi < 2,
                    jnp.where(rowi < 1, prev[SUBLANES - 2:SUBLANES - 1, :],
                              prev[SUBLANES - 1:SUBLANES, :]),
                    pltpu.roll(cx, 2, axis=0))
    conv = convb_ref[...] + convw_ref[0:1, :] * cx2
    conv = conv + convw_ref[1:2, :] * cx1
    conv = conv + convw_ref[2:3, :] * cx
    cxprev_s[...] = cx[TM_MIX - SUBLANES:, :]
    cxlast_ref[...] = cx[TM_MIX - SUBLANES:, :]
    bx = (proj(OFF_BG, D_MODEL) * conv).astype(BF16)
    sa_rows = []
    for c in range(nblk):
        sa_rows.append(jnp.concatenate(
            [_dot(wsp_s[g], va[c * BLK:(c + 1) * BLK, g * LANES:(g + 1) * LANES])
             for g in range(A_GROUPS)], axis=1) + bsp_ref[...])
    ax = (ua * jnp.concatenate(sa_rows, axis=0)).astype(BF16)

    cos_a, sin_a = cos_a_ref[...], sin_a_ref[...]
    cos_t = cos_a * cos_b_ref[...] - sin_a * sin_b_ref[...]
    sin_s = sin_a * cos_bs_ref[...] + cos_a * sin_bs_ref[...]
    q = (_rope(proj(OFF_Q, N_HEADS * HEAD_DIM), cos_t, sin_s) * (HEAD_DIM ** -0.5)).astype(BF16)
    k = _rope(proj(OFF_K, KV_DIM), cos_t, sin_s)
    v = proj(OFF_V, KV_DIM)
    klast_ref[...] = k[TM_MIX - BLK:, :]
    vlast_ref[...] = v[TM_MIX - BLK:, :]

    merged_ab, gate_c = [], []

    def gated_branch_pieces():
        for j in range(D_MODEL // GATE_COLS):
            cols = slice(j * GATE_COLS, (j + 1) * GATE_COLS)
            y_a = _dot(ax, wa_ref[:, cols])
            yield
            part = jax.nn.sigmoid(proj(OFF_GA + j * GATE_COLS, GATE_COLS)) * y_a
            yield
            y_b = _dot(bx, wb_ref[:, cols])
            yield
            merged_ab.append(part + jax.nn.sigmoid(proj(OFF_GB + j * GATE_COLS, GATE_COLS)) * y_b)
            yield
            gate_c.append(jax.nn.sigmoid(proj(OFF_GC + j * GATE_COLS, GATE_COLS)))
            yield

    pieces = gated_branch_pieces()
    n_pieces = 5 * (D_MODEL // GATE_COLS)
    for _ in range(PIECES_BEFORE_ATTN):
        next(pieces)

    qi = jax.lax.broadcasted_iota(jnp.int32, (BLK, 2 * BLK), 0)
    kj = jax.lax.broadcasted_iota(jnp.int32, (BLK, 2 * BLK), 1)
    band = (kj >= qi) & (kj <= qi + WINDOW)
    lane_low = _lane_iota((BLK, LANES)) < HEAD_DIM

    k_parts = [[(kprev_s[g, 0], kprev_s[g, 1]) for g in range(N_KV_HEADS)]]
    v_parts = [[(vprev_s[g, 0], vprev_s[g, 1]) for g in range(N_KV_HEADS)]]
    for c in range(nblk):
        k_parts.append(_split_kv_heads(k[c * BLK:(c + 1) * BLK, :]))
        v_parts.append(_split_kv_heads(v[c * BLK:(c + 1) * BLK, :]))
    for g in range(N_KV_HEADS):
        kprev_s[g, 0] = k_parts[nblk][g][0]
        kprev_s[g, 1] = k_parts[nblk][g][1]
        vprev_s[g, 0] = v_parts[nblk][g][0]
        vprev_s[g, 1] = v_parts[nblk][g][1]
    mask_first = band & (kj >= jnp.where(step == 0, BLK, 0))

    def block_diag(parts, c, g):
        return jnp.concatenate([parts[c][g][0], parts[c + 1][g][0],
                                parts[c][g][1], parts[c + 1][g][1]], axis=0)

    def scores(c, m):
        kbd = block_diag(k_parts, c, (2 * m) // Q_PER_KV)
        return _dot_nt(q[c * BLK:(c + 1) * BLK, m * LANES:(m + 1) * LANES], kbd)

    def softmax_numerators(c, m, s):
        mask = mask_first if c == 0 else band
        es, invs = [], []
        for hh in range(2):
            sink = sinks_ref[layer, 2 * m + hh]
            sh = jnp.where(mask, s[:, hh * 2 * BLK:(hh + 1) * 2 * BLK], NEG_INF)
            mx = jnp.maximum(jnp.max(sh, axis=-1, keepdims=True), sink)
            e = jnp.exp(sh - mx)
            den = jnp.sum(e, axis=-1, keepdims=True) + jnp.exp(sink - mx)
            es.append(e.astype(BF16))
            invs.append(1.0 / den)
        return jnp.concatenate(es, axis=1), jnp.where(lane_low, invs[0], invs[1])

    def weighted_values(c, m, e, inv):
        vbd = block_diag(v_parts, c, (2 * m) // Q_PER_KV)
        return _dot(e, vbd) * inv

    its = [(c, m) for c in range(nblk) for m in range(N_HEADS // 2)]
    s_queue = [scores(*its[i]) for i in range(min(ATTN_LOOKAHEAD, len(its)))]
    o_parts = {}
    issued = PIECES_BEFORE_ATTN
    for i, (c, m) in enumerate(its):
        if i + ATTN_LOOKAHEAD < len(its):
            s_queue.append(scores(*its[i + ATTN_LOOKAHEAD]))
        e, inv = softmax_numerators(c, m, s_queue.pop(0))
        o_parts[(c, m)] = weighted_values(c, m, e, inv)
        while issued < PIECES_BEFORE_ATTN + ((i + 1) * (n_pieces - PIECES_BEFORE_ATTN)) // len(its):
            next(pieces)
            issued += 1
    o = jnp.concatenate(
        [jnp.concatenate([o_parts[(c, m)] for m in range(N_HEADS // 2)], axis=1)
         for c in range(nblk)], axis=0).astype(BF16)
    y_c = _dot(o, wc_ref[...])
    merged = jnp.concatenate(merged_ab, axis=1) + jnp.concatenate(gate_c, axis=1) * y_c

    merged = merged.astype(BF16)
    for c in range(nblk):
        rows = slice(c * BLK, (c + 1) * BLK)
        x1_ref[rows, :] = x[rows, :] + _rmsnorm(_dot(merged[rows, :], wo_ref[...]), gpost_ref[...])


def _prompt_mixer(layer, x, rope, sinks, gpre, gpost, w_in, lng, lnb, wsp, bsp, convw, convb,
                  wa, wb, wc, wo):
    n = x.shape[0]
    grid = (n // TM_MIX,)
    lb = functools.partial(_layer_block, layer)
    in_specs = [
        pl.BlockSpec((TM_MIX, D_MODEL), lambda i, *_: (i, 0)),
        pl.BlockSpec((None, 1, LANES), lambda i, *_: (i, 0, 0)),
        pl.BlockSpec((None, 1, LANES), lambda i, *_: (i, 0, 0)),
        _whole((TM_MIX, LANES)), _whole((TM_MIX, LANES)),
        _whole((TM_MIX, LANES)), _whole((TM_MIX, LANES)),
        lb((1, D_MODEL)), lb((1, D_MODEL)),
        lb((D_MODEL, IN_WIDTH)),
        lb((1, D_MODEL)), lb((1, D_MODEL)),
        lb((A_GROUPS, CHUNK, CHUNK)), lb((CHUNK, D_MODEL)),
        lb((CONV_W, D_MODEL)), lb((1, D_MODEL)),
        lb((D_MODEL, D_MODEL)), lb((D_MODEL, D_MODEL)),
        lb((D_MODEL, D_MODEL)), lb((D_MODEL, D_MODEL)),
    ]
    out_specs = [
        pl.BlockSpec((TM_MIX, D_MODEL), lambda i, *_: (i, 0)),
        pl.BlockSpec((SUBLANES, D_MODEL), lambda i, *_: (0, 0)),
        pl.BlockSpec((BLK, KV_DIM), lambda i, *_: (0, 0)),
        pl.BlockSpec((BLK, KV_DIM), lambda i, *_: (0, 0)),
    ]
    out_shape = [
        jax.ShapeDtypeStruct((n, D_MODEL), F32),
        jax.ShapeDtypeStruct((SUBLANES, D_MODEL), F32),
        jax.ShapeDtypeStruct((BLK, KV_DIM), F32),
        jax.ShapeDtypeStruct((BLK, KV_DIM), F32),
    ]
    scratch = [
        pltpu.VMEM((A_GROUPS, CHUNK, CHUNK), BF16),
        pltpu.VMEM((N_KV_HEADS, 2, BLK, LANES), BF16),
        pltpu.VMEM((N_KV_HEADS, 2, BLK, LANES), BF16),
        pltpu.VMEM((SUBLANES, D_MODEL), F32),
    ]
    return pl.pallas_call(
        functools.partial(_mixer_kernel, layer),
        grid_spec=pltpu.PrefetchScalarGridSpec(
            num_scalar_prefetch=1, grid=grid, in_specs=in_specs, out_specs=out_specs,
            scratch_shapes=scratch),
        out_shape=out_shape,
        compiler_params=pltpu.CompilerParams(
            dimension_semantics=("arbitrary",), vmem_limit_bytes=V7X_VMEM_LIMIT),
        name="prompt_mixer",
    )(sinks, x, *rope, gpre, gpost, w_in, lng, lnb, wsp, bsp, convw, convb, wa, wb, wc, wo)


def _swiglu(x, gpre, gpost, wg_ref, wu_ref, wd_ref):
    h = _rmsnorm(x, gpre).astype(BF16)
    a = jax.nn.silu(_dot(h, wg_ref[...])) * _dot(h, wu_ref[...])
    return x + _rmsnorm(_dot(a.astype(BF16), wd_ref[...]), gpost)


def _ffn_kernel(x_ref, gpre_ref, gpost_ref, wg_ref, wu_ref, wd_ref, y_ref):
    y_ref[...] = _swiglu(x_ref[...], gpre_ref[...], gpost_ref[...], wg_ref, wu_ref, wd_ref)


def _prompt_ffn(layer, x, gpre, gpost, wg, wu, wd):
    n = x.shape[0]
    lb = functools.partial(_layer_block, layer)
    row_tile = pl.BlockSpec((TM_FFN, D_MODEL), lambda i: (i, 0))
    return pl.pallas_call(
        _ffn_kernel,
        grid=(n // TM_FFN,),
        in_specs=[row_tile, lb((1, D_MODEL)), lb((1, D_MODEL)),
                  lb((D_MODEL, D_FF)), lb((D_MODEL, D_FF)), lb((D_FF, D_MODEL))],
        out_specs=row_tile,
        out_shape=jax.ShapeDtypeStruct((n, D_MODEL), F32),
        compiler_params=pltpu.CompilerParams(
            dimension_semantics=("parallel",), vmem_limit_bytes=V7X_VMEM_LIMIT),
        name="prompt_ffn",
    )(x, gpre, gpost, wg, wu, wd)


def _sample_proj_kernel(x_ref, hist_ref, cos_ref, sin_ref, gpre_ref, w_in_ref, lng_ref, lnb_ref,
                        wsp0_ref, bsp0_ref, convw_ref, convb_ref, wa_ref, wb_ref,
                        va_ref, newconv_ref, q_ref, k_ref, v_ref, mab_ref, gc_ref):
    x = x_ref[...]
    h = _rmsnorm(x, gpre_ref[...]).astype(BF16)

    def proj(off, width):
        return _dot(h, w_in_ref[:, off:off + width])

    ua = jax.nn.gelu(proj(OFF_UA, D_MODEL))
    va = _layernorm(jax.nn.gelu(proj(OFF_VA, D_MODEL)), lng_ref[...], lnb_ref[...])
    va_ref[...] = va
    sa = wsp0_ref[...] * va + bsp0_ref[...]
    y_a = _dot((ua * sa).astype(BF16), wa_ref[...])
    merged = jax.nn.sigmoid(proj(OFF_GA, D_MODEL)) * y_a

    cx = proj(OFF_CG, D_MODEL) * proj(OFF_HB, D_MODEL)
    h0 = hist_ref[:, 0:D_MODEL]
    h1 = hist_ref[:, D_MODEL:2 * D_MODEL]
    conv = convb_ref[...] + convw_ref[0:1, :] * h0
    conv = conv + convw_ref[1:2, :] * h1
    conv = conv + convw_ref[2:3, :] * cx
    newconv_ref[:, 0:D_MODEL] = h1
    newconv_ref[:, D_MODEL:2 * D_MODEL] = cx
    y_b = _dot((proj(OFF_BG, D_MODEL) * conv).astype(BF16), wb_ref[...])
    mab_ref[...] = merged + jax.nn.sigmoid(proj(OFF_GB, D_MODEL)) * y_b
    gc_ref[...] = jax.nn.sigmoid(proj(OFF_GC, D_MODEL))

    cos_t = cos_ref[...]
    sin_s = sin_ref[...]
    q_ref[...] = _rope(proj(OFF_Q, N_HEADS * HEAD_DIM), cos_t, sin_s) * (HEAD_DIM ** -0.5)
    k_ref[...] = _rope(proj(OFF_K, KV_DIM), cos_t, sin_s)
    v_ref[...] = proj(OFF_V, KV_DIM)


def _sample_proj(layer, x, hist, cos_t, sin_s, gpre, w_in, lng, lnb, wsp0, bsp0, convw, convb,
                 wa, wb):
    b = x.shape[0]
    lb = functools.partial(_layer_block, layer)
    in_specs = [
        _whole(x.shape), lb((b, (CONV_W - 1) * D_MODEL)), _whole(cos_t.shape), _whole(sin_s.shape),
        lb((1, D_MODEL)), lb((D_MODEL, IN_WIDTH)), lb((1, D_MODEL)), lb((1, D_MODEL)),
        lb((1, D_MODEL)), lb((1, D_MODEL)), lb((CONV_W, D_MODEL)), lb((1, D_MODEL)),
        lb((D_MODEL, D_MODEL)), lb((D_MODEL, D_MODEL)),
    ]
    out_shape = [
        jax.ShapeDtypeStruct((b, D_MODEL), F32),
        jax.ShapeDtypeStruct((b, 2 * D_MODEL), F32),
        jax.ShapeDtypeStruct((b, D_MODEL), F32),
        jax.ShapeDtypeStruct((b, KV_DIM), F32),
        jax.ShapeDtypeStruct((b, KV_DIM), F32),
        jax.ShapeDtypeStruct((b, D_MODEL), F32),
        jax.ShapeDtypeStruct((b, D_MODEL), F32),
    ]
    return pl.pallas_call(
        _sample_proj_kernel,
        grid=(1,),
        in_specs=in_specs,
        out_specs=[pl.BlockSpec(s.shape, lambda i: (0, 0)) for s in out_shape],
        out_shape=out_shape,
        compiler_params=pltpu.CompilerParams(
            dimension_semantics=("arbitrary",), vmem_limit_bytes=V7X_VMEM_LIMIT),
        name="sample_proj",
    )(x, hist, cos_t, sin_s, gpre, w_in, lng, lnb, wsp0, bsp0, convw, convb, wa, wb)


def _sample_attn_kernel(layer, n_aliased, sinks_ref, *refs):
    (q_ref, knew_ref, vnew_ref, knew_t_ref, vnew_t_ref, kc_ref, vc_ref,
     o_ref, kout_ref, vout_ref) = refs[n_aliased:]
    step = pl.program_id(0)
    qf = q_ref[...]
    s = jnp.einsum('grd,gdk->grk', qf.astype(BF16), kc_ref[...].astype(BF16),
                   preferred_element_type=F32)
    s_new = jnp.sum(qf * knew_ref[...], axis=-1, keepdims=True)
    gidx = jax.lax.broadcasted_iota(jnp.int32, (S2_GROUPS, Q_PER_KV, 1), 0)
    ridx = jax.lax.broadcasted_iota(jnp.int32, (S2_GROUPS, Q_PER_KV, 1), 1)
    head = (gidx % N_KV_HEADS) * Q_PER_KV + ridx
    sink = jnp.zeros((S2_GROUPS, Q_PER_KV, 1), F32)
    for hh in range(N_HEADS):
        sink = jnp.where(head == hh, sinks_ref[layer, hh], sink)
    mx = jnp.maximum(jnp.maximum(jnp.max(s, axis=-1, keepdims=True), s_new), sink)
    e = jnp.exp(s - mx)
    e_new = jnp.exp(s_new - mx)
    den = jnp.sum(e, axis=-1, keepdims=True) + e_new + jnp.exp(sink - mx)
    o = jnp.einsum('grk,gdk->grd', e.astype(BF16), vc_ref[...].astype(BF16),
                   preferred_element_type=F32)
    o_ref[...] = (o + e_new * vnew_ref[...]) * (1.0 / den)

    steps_per_tile = LANES // S2_GROUPS
    tile = pl.multiple_of((step // steps_per_tile) * LANES, LANES)
    first_col = (step % steps_per_tile) * S2_GROUPS
    last_lane = _lane_iota((HEAD_DIM, WINDOW)) == WINDOW - 1
    for new_t_ref, cache_ref, out_ref in ((knew_t_ref, kc_ref, kout_ref),
                                          (vnew_t_ref, vc_ref, vout_ref)):
        new_cols = new_t_ref[:, pl.ds(tile, LANES)]
        for j in range(S2_GROUPS):
            col = pltpu.roll(new_cols, WINDOW - 1 - first_col - j, axis=1)
            out_ref[j] = jnp.where(last_lane, col, pltpu.roll(cache_ref[j], WINDOW - 1, axis=1))


def _sample_attn(layer, sinks, q, knew, vnew, knew_t, vnew_t, kcache, vcache, kout_prev, vout_prev):
    groups = q.shape[0]
    group3 = lambda d1, d2: pl.BlockSpec((S2_GROUPS, d1, d2), lambda i, *_: (i, 0, 0))
    cache_blk = pl.BlockSpec((None, S2_GROUPS, HEAD_DIM, WINDOW), lambda i, *_: (layer, i, 0, 0))
    aliased = [] if kout_prev is None else [kout_prev, vout_prev]
    n_al = len(aliased)
    in_specs = ([pl.BlockSpec(memory_space=pl.ANY)] * n_al +
                [group3(Q_PER_KV, HEAD_DIM), group3(1, HEAD_DIM), group3(1, HEAD_DIM),
                 _whole(knew_t.shape), _whole(vnew_t.shape), cache_blk, cache_blk])
    io_alias = {1: 1, 2: 2} if n_al else {}
    return pl.pallas_call(
        functools.partial(_sample_attn_kernel, layer, n_al),
        grid_spec=pltpu.PrefetchScalarGridSpec(
            num_scalar_prefetch=1, grid=(groups // S2_GROUPS,),
            in_specs=in_specs,
            out_specs=[group3(Q_PER_KV, HEAD_DIM), cache_blk, cache_blk]),
        out_shape=[jax.ShapeDtypeStruct((groups, Q_PER_KV, HEAD_DIM), F32),
                   jax.ShapeDtypeStruct(kcache.shape, F32),
                   jax.ShapeDtypeStruct(vcache.shape, F32)],
        input_output_aliases=io_alias,
        compiler_params=pltpu.CompilerParams(
            dimension_semantics=("arbitrary",), vmem_limit_bytes=V7X_VMEM_LIMIT),
        name="sample_attn",
    )(sinks, *aliased, q, knew, vnew, knew_t, vnew_t, kcache, vcache)


def _sample_out_kernel(x_ref, o_ref, mab_ref, gc_ref, gpost_ref, gpre2_ref, gpost2_ref,
                       wc_ref, wo_ref, wg_ref, wu_ref, wd_ref, y_ref):
    y_c = _dot(o_ref[...].astype(BF16), wc_ref[...])
    merged = mab_ref[...] + gc_ref[...] * y_c
    x1 = x_ref[...] + _rmsnorm(_dot(merged.astype(BF16), wo_ref[...]), gpost_ref[...])
    y_ref[...] = _swiglu(x1, gpre2_ref[...], gpost2_ref[...], wg_ref, wu_ref, wd_ref)


def _sample_out(layer, x, o, mab, gc, gpost, gpre2, gpost2, wc, wo, wg, wu, wd):
    lb = functools.partial(_layer_block, layer)
    in_specs = [_whole(x.shape), _whole(o.shape), _whole(mab.shape), _whole(gc.shape),
                lb((1, D_MODEL)), lb((1, D_MODEL)), lb((1, D_MODEL)),
                lb((D_MODEL, D_MODEL)), lb((D_MODEL, D_MODEL)),
                lb((D_MODEL, D_FF)), lb((D_MODEL, D_FF)), lb((D_FF, D_MODEL))]
    return pl.pallas_call(
        _sample_out_kernel,
        grid=(1,),
        in_specs=in_specs,
        out_specs=pl.BlockSpec(x.shape, lambda i: (0, 0)),
        out_shape=jax.ShapeDtypeStruct(x.shape, F32),
        compiler_params=pltpu.CompilerParams(
            dimension_semantics=("arbitrary",), vmem_limit_bytes=V7X_VMEM_LIMIT),
        name="sample_out",
    )(x, o, mab, gc, gpost, gpre2, gpost2, wc, wo, wg, wu, wd)


def _rope_angles(pos):
    half = HEAD_DIM // 2
    inv = jnp.power(jnp.float32(ROPE_THETA), -jnp.arange(half, dtype=F32) * (2.0 / HEAD_DIM))
    return pos.astype(F32)[:, None] * jnp.tile(inv, LANES // half)[None, :]


def _rope_sign():
    lane = jnp.arange(LANES, dtype=jnp.int32)
    return jnp.where((lane % HEAD_DIM) < (HEAD_DIM // 2), -1.0, 1.0).astype(F32)[None, :]


def kernel(x_prompt, x_sample, state_conv, cache_win_k, cache_win_v, norm_pre_mix, norm_post_mix,
           norm_pre_ffn, norm_post_ffn, w_in, chunk_ln_g, chunk_ln_b, w_spatial, b_spatial, conv_w,
           conv_b, attn_sinks, w_br_a, w_br_b, w_br_c, w_out, w_ffn_gate, w_ffn_up, w_ffn_down):
    xp = x_prompt.reshape(SEQ, D_MODEL)
    xs = x_sample.reshape(DEC_BATCH, D_MODEL)

    sign = _rope_sign()
    ang_a = _rope_angles(jnp.arange(SEQ // TM_MIX, dtype=jnp.int32) * TM_MIX)
    ang_b = _rope_angles(jnp.arange(TM_MIX, dtype=jnp.int32))
    rope_p = (jnp.cos(ang_a)[:, None, :], jnp.sin(ang_a)[:, None, :],
              jnp.cos(ang_b), jnp.sin(ang_b), sign * jnp.cos(ang_b), sign * jnp.sin(ang_b))
    ang_s = _rope_angles(jnp.full((1,), PAST_LEN, dtype=jnp.int32))
    cos_s, sin_s = jnp.cos(ang_s), sign * jnp.sin(ang_s)

    w_in_b = w_in.astype(BF16)
    wa, wb, wc, wo = (w_br_a.astype(BF16), w_br_b.astype(BF16), w_br_c.astype(BF16),
                      w_out.astype(BF16))
    wg, wu, wd = w_ffn_gate.astype(BF16), w_ffn_up.astype(BF16), w_ffn_down.astype(BF16)
    rows = lambda a: a.reshape(DEPTH, 1, -1)
    gpre, gpost = rows(norm_pre_mix), rows(norm_post_mix)
    gpre2, gpost2 = rows(norm_pre_ffn), rows(norm_post_ffn)
    lng, lnb, convb = rows(chunk_ln_g), rows(chunk_ln_b), rows(conv_b)
    bsp = jnp.repeat(jnp.transpose(b_spatial, (0, 2, 1)), A_GROUP_DIM, axis=2)
    wsp0 = rows(jnp.repeat(w_spatial[:, :, 0, 0], A_GROUP_DIM, axis=1))
    bsp0 = rows(jnp.repeat(b_spatial[:, :, 0], A_GROUP_DIM, axis=1))
    hist = state_conv.reshape(DEPTH, DEC_BATCH, (CONV_W - 1) * D_MODEL)
    n_groups = DEC_BATCH * N_KV_HEADS
    to_groups = lambda c: jnp.transpose(c, (0, 1, 3, 4, 2)).reshape(DEPTH, n_groups, HEAD_DIM, WINDOW)
    kcache, vcache = to_groups(cache_win_k), to_groups(cache_win_v)

    p_conv, p_k, p_v = [], [], []
    s_conv, s_cv = [], []
    knext = vnext = None
    for l in range(DEPTH):
        x1, cxlast, klast, vlast = _prompt_mixer(
            l, xp, rope_p, attn_sinks, gpre, gpost, w_in_b, lng, lnb, w_spatial, bsp,
            conv_w, convb, wa, wb, wc, wo)
        xp = _prompt_ffn(l, x1, gpre2, gpost2, wg, wu, wd)
        p_conv.append(cxlast[SUBLANES - (CONV_W - 1):].reshape(1, CONV_W - 1, D_MODEL))
        p_k.append(klast.reshape(1, WINDOW, N_KV_HEADS, HEAD_DIM))
        p_v.append(vlast.reshape(1, WINDOW, N_KV_HEADS, HEAD_DIM))

        va, newconv, q, knew, vnew, mab, gc = _sample_proj(
            l, xs, hist, cos_s, sin_s, gpre, w_in_b, lng, lnb, wsp0, bsp0, conv_w, convb, wa, wb)
        knew = knew.reshape(n_groups, HEAD_DIM)
        vnew = vnew.reshape(n_groups, HEAD_DIM)
        o, knext, vnext = _sample_attn(
            l, attn_sinks, q.reshape(n_groups, Q_PER_KV, HEAD_DIM),
            knew[:, None, :], vnew[:, None, :], jnp.transpose(knew), jnp.transpose(vnew),
            kcache, vcache, knext, vnext)
        xs = _sample_out(l, xs, o.reshape(DEC_BATCH, D_MODEL), mab, gc, gpost, gpre2, gpost2,
                         wc, wo, wg, wu, wd)
        s_conv.append(newconv.reshape(DEC_BATCH, CONV_W - 1, D_MODEL))
        s_cv.append(va.reshape(DEC_BATCH, 1, D_MODEL))

    from_groups = lambda c: jnp.transpose(
        c.reshape(DEPTH, DEC_BATCH, N_KV_HEADS, HEAD_DIM, WINDOW), (0, 1, 4, 2, 3))
    return (xp.reshape(1, SEQ, D_MODEL), xs.reshape(DEC_BATCH, 1, D_MODEL),
            jnp.stack(p_conv), jnp.stack(p_k), jnp.stack(p_v),
            jnp.stack(s_conv), from_groups(knext), from_groups(vnext), jnp.stack(s_cv))
```

```python
import functools

import jax
import jax.numpy as jnp
from jax.experimental import pallas as pl
from jax.experimental.pallas import tpu as pltpu

D_MODEL = 1024
SEQ = 16384
DEPTH = 2
DEC_BATCH = 128
PAST_LEN = 16384
CHUNK = 128
A_GROUPS = 8
A_GROUP_DIM = D_MODEL // A_GROUPS
CONV_W = 3
N_HEADS = 16
N_KV_HEADS = 4
HEAD_DIM = 64
Q_PER_KV = N_HEADS // N_KV_HEADS
WINDOW = 128
ROPE_THETA = 10000.0
D_FF = 2816
KV_DIM = N_KV_HEADS * HEAD_DIM
NEG_INF = -1e30

OFF_UA = 0
OFF_VA = OFF_UA + D_MODEL
OFF_BG = OFF_VA + D_MODEL
OFF_CG = OFF_BG + D_MODEL
OFF_HB = OFF_CG + D_MODEL
OFF_Q = OFF_HB + D_MODEL
OFF_K = OFF_Q + N_HEADS * HEAD_DIM
OFF_V = OFF_K + KV_DIM
OFF_GA = OFF_V + KV_DIM
OFF_GB = OFF_GA + D_MODEL
OFF_GC = OFF_GB + D_MODEL
IN_WIDTH = OFF_GC + D_MODEL

LANES = 128
SUBLANES = 8
BF16_SUBLANES = 16
V7X_VMEM_LIMIT = 56 * 1024 * 1024

TM_MIX = 256
TM_FFN = 512
FFN_ROWS = 256
BLK = WINDOW
S2_GROUPS = 32
ATTN_LOOKAHEAD = 2
GATE_COLS = 256
PIECES_BEFORE_ATTN = 3

assert CHUNK == BLK and WINDOW == BLK and PAST_LEN >= WINDOW
assert 2 * HEAD_DIM == LANES and A_GROUP_DIM == LANES
assert SEQ % TM_MIX == 0 and SEQ % TM_FFN == 0 and TM_MIX % BLK == 0
assert LANES % S2_GROUPS == 0 and S2_GROUPS % N_KV_HEADS == 0 and WINDOW == LANES
assert (DEC_BATCH * N_KV_HEADS) % LANES == 0

BF16 = jnp.bfloat16
F32 = jnp.float32


def _dot(a, b):
    return jnp.dot(a, b, preferred_element_type=F32)


def _dot_nt(a, b):
    return jax.lax.dot_general(a, b, (((1,), (1,)), ((), ())), preferred_element_type=F32)


def _rmsnorm(x, g, eps=1e-6):
    return x * jax.lax.rsqrt(jnp.mean(x * x, axis=-1, keepdims=True) + eps) * g


def _layernorm(x, g, b, eps=1e-5):
    mu = jnp.mean(x, axis=-1, keepdims=True)
    xc = x - mu
    var = jnp.mean(xc * xc, axis=-1, keepdims=True)
    return xc * jax.lax.rsqrt(var + eps) * g + b


def _lane_iota(shape):
    return jax.lax.broadcasted_iota(jnp.int32, shape, len(shape) - 1)


def _rope_block(xb, cos_t, sin_s):
    lane = _lane_iota(xb.shape)
    first_half = (lane % HEAD_DIM) < (HEAD_DIM // 2)
    rot = jnp.where(first_half,
                    pltpu.roll(xb, LANES - HEAD_DIM // 2, axis=1),
                    pltpu.roll(xb, HEAD_DIM // 2, axis=1))
    return xb * cos_t + rot * sin_s


def _rope(x, cos_t, sin_s):
    n = x.shape[1] // LANES
    return jnp.concatenate(
        [_rope_block(x[:, j * LANES:(j + 1) * LANES], cos_t, sin_s) for j in range(n)], axis=1)


def _split_kv_heads(blk):
    out = []
    for j in range(KV_DIM // LANES):
        b = blk[:, j * LANES:(j + 1) * LANES]
        r = pltpu.roll(b, HEAD_DIM, axis=1)
        low = _lane_iota(b.shape) < HEAD_DIM
        zero = jnp.zeros_like(b)
        out.append((jnp.where(low, b, zero).astype(BF16), jnp.where(low, zero, r).astype(BF16)))
        out.append((jnp.where(low, r, zero).astype(BF16), jnp.where(low, zero, b).astype(BF16)))
    return out


def _layer_block(layer, shape, single_buffer=True):
    nd = len(shape)
    mode = pl.Buffered(1) if single_buffer else None
    return pl.BlockSpec((None,) + tuple(shape), lambda *_: (layer,) + (0,) * nd, pipeline_mode=mode)


def _whole(shape):
    nd = len(shape)
    return pl.BlockSpec(tuple(shape), lambda *_: (0,) * nd, pipeline_mode=pl.Buffered(1))


N_MIXER_INPUTS = 20


def _cast_plan(rows, n_steps):
    for nb in range(n_steps, 0, -1):
        if rows % nb == 0 and (rows // nb) % BF16_SUBLANES == 0:
            return nb
    raise ValueError(f"no bf16-aligned row split of {rows} rows over {n_steps} steps")


def _cast_specs(layer, w_stacked, n_steps):
    _, rows, cols = w_stacked.shape
    nb = _cast_plan(rows, n_steps)
    blk = lambda i: jnp.minimum(i, nb - 1)
    return (pl.BlockSpec((None, rows // nb, cols), lambda i, *_: (layer, blk(i), 0)),
            pl.BlockSpec((rows // nb, cols), lambda i, *_: (blk(i), 0)),
            jax.ShapeDtypeStruct((rows, cols), BF16))


def _cast_row_blocks(src_refs, dst_refs):
    for src, dst in zip(src_refs, dst_refs):
        dst[...] = src[...].astype(BF16)


def _mixer_kernel(layer, n_cast, sinks_ref, *refs):
    (x_ref, cos_a_ref, sin_a_ref, cos_b_ref, sin_b_ref, cos_bs_ref, sin_bs_ref,
     gpre_ref, gpost_ref, w_in_ref, lng_ref, lnb_ref, wsp_ref, bsp_ref, convw_ref, convb_ref,
     wa_ref, wb_ref, wc_ref, wo_ref) = refs[:N_MIXER_INPUTS]
    cast_src = refs[N_MIXER_INPUTS:N_MIXER_INPUTS + n_cast]
    outs = refs[N_MIXER_INPUTS + n_cast:]
    x1_ref, cxlast_ref, klast_ref, vlast_ref = outs[:4]
    cast_dst = outs[4:4 + n_cast]
    wsp_s, kprev_s, vprev_s, cxprev_s = outs[4 + n_cast:]
    step = pl.program_id(0)
    nblk = TM_MIX // BLK

    @pl.when(step == 0)
    def _init():
        row = jax.lax.broadcasted_iota(jnp.int32, (CHUNK, CHUNK), 0)
        col = jax.lax.broadcasted_iota(jnp.int32, (CHUNK, CHUNK), 1)
        for g in range(A_GROUPS):
            wsp_s[g] = jnp.where(col <= row, wsp_ref[g], 0.0).astype(BF16)
        kprev_s[...] = jnp.zeros_like(kprev_s)
        vprev_s[...] = jnp.zeros_like(vprev_s)
        cxprev_s[...] = jnp.zeros_like(cxprev_s)

    x = x_ref[...]
    h = _rmsnorm(x, gpre_ref[...]).astype(BF16)

    def proj(off, width):
        return _dot(h, w_in_ref[:, off:off + width])

    va = _layernorm(jax.nn.gelu(proj(OFF_VA, D_MODEL)), lng_ref[...], lnb_ref[...]).astype(BF16)
    ua = jax.nn.gelu(proj(OFF_UA, D_MODEL))
    cx = proj(OFF_CG, D_MODEL) * proj(OFF_HB, D_MODEL)
    _cast_row_blocks(cast_src, cast_dst)
    prev = cxprev_s[...]
    rowi = jax.lax.broadcasted_iota(jnp.int32, cx.shape, 0)
    cx1 = jnp.where(rowi < 1, prev[SUBLANES - 1:SUBLANES, :], pltpu.roll(cx, 1, axis=0))
    cx2 = jnp.where(rowi < 2,
                    jnp.where(rowi < 1, prev[SUBLANES - 2:SUBLANES - 1, :],
                              prev[SUBLANES - 1:SUBLANES, :]),
                    pltpu.roll(cx, 2, axis=0))
    conv = convb_ref[...] + convw_ref[0:1, :] * cx2
    conv = conv + convw_ref[1:2, :] * cx1
    conv = conv + convw_ref[2:3, :] * cx
    cxprev_s[...] = cx[TM_MIX - SUBLANES:, :]
    cxlast_ref[...] = cx[TM_MIX - SUBLANES:, :]
    bx = (proj(OFF_BG, D_MODEL) * conv).astype(BF16)
    sa_rows = []
    for c in range(nblk):
        sa_rows.append(jnp.concatenate(
            [_dot(wsp_s[g], va[c * BLK:(c + 1) * BLK, g * LANES:(g + 1) * LANES])
             for g in range(A_GROUPS)], axis=1) + bsp_ref[...])
    ax = (ua * jnp.concatenate(sa_rows, axis=0)).astype(BF16)

    cos_a, sin_a = cos_a_ref[...], sin_a_ref[...]
    cos_t = cos_a * cos_b_ref[...] - sin_a * sin_b_ref[...]
    sin_s = sin_a * cos_bs_ref[...] + cos_a * sin_bs_ref[...]
    q = (_rope(proj(OFF_Q, N_HEADS * HEAD_DIM), cos_t, sin_s) * (HEAD_DIM ** -0.5)).astype(BF16)
    k = _rope(proj(OFF_K, KV_DIM), cos_t, sin_s)
    v = proj(OFF_V, KV_DIM)
    klast_ref[...] = k[TM_MIX - BLK:, :]
    vlast_ref[...] = v[TM_MIX - BLK:, :]

    merged_ab, gate_c = [], []

    def gated_branch_pieces():
        for j in range(D_MODEL // GATE_COLS):
            cols = slice(j * GATE_COLS, (j + 1) * GATE_COLS)
            y_a = _dot(ax, wa_ref[:, cols])
            yield
            part = jax.nn.sigmoid(proj(OFF_GA + j * GATE_COLS, GATE_COLS)) * y_a
            yield
            y_b = _dot(bx, wb_ref[:, cols])
            yield
            merged_ab.append(part + jax.nn.sigmoid(proj(OFF_GB + j * GATE_COLS, GATE_COLS)) * y_b)
            yield
            gate_c.append(jax.nn.sigmoid(proj(OFF_GC + j * GATE_COLS, GATE_COLS)))
            yield

    pieces = gated_branch_pieces()
    n_pieces = 5 * (D_MODEL // GATE_COLS)
    for _ in range(PIECES_BEFORE_ATTN):
        next(pieces)

    qi = jax.lax.broadcasted_iota(jnp.int32, (BLK, 2 * BLK), 0)
    kj = jax.lax.broadcasted_iota(jnp.int32, (BLK, 2 * BLK), 1)
    band = (kj >= qi) & (kj <= qi + WINDOW)
    lane_low = _lane_iota((BLK, LANES)) < HEAD_DIM

    k_parts = [[(kprev_s[g, 0], kprev_s[g, 1]) for g in range(N_KV_HEADS)]]
    v_parts = [[(vprev_s[g, 0], vprev_s[g, 1]) for g in range(N_KV_HEADS)]]
    for c in range(nblk):
        k_parts.append(_split_kv_heads(k[c * BLK:(c + 1) * BLK, :]))
        v_parts.append(_split_kv_heads(v[c * BLK:(c + 1) * BLK, :]))
    for g in range(N_KV_HEADS):
        kprev_s[g, 0] = k_parts[nblk][g][0]
        kprev_s[g, 1] = k_parts[nblk][g][1]
        vprev_s[g, 0] = v_parts[nblk][g][0]
        vprev_s[g, 1] = v_parts[nblk][g][1]
    mask_first = band & (kj >= jnp.where(step == 0, BLK, 0))

    def block_diag(parts, c, g):
        return jnp.concatenate([parts[c][g][0], parts[c + 1][g][0],
                                parts[c][g][1], parts[c + 1][g][1]], axis=0)

    def scores(c, m):
        kbd = block_diag(k_parts, c, (2 * m) // Q_PER_KV)
        return _dot_nt(q[c * BLK:(c + 1) * BLK, m * LANES:(m + 1) * LANES], kbd)

    def softmax_numerators(c, m, s):
        mask = mask_first if c == 0 else band
        es, invs = [], []
        for hh in range(2):
            sink = sinks_ref[layer, 2 * m + hh]
            sh = jnp.where(mask, s[:, hh * 2 * BLK:(hh + 1) * 2 * BLK], NEG_INF)
            mx = jnp.maximum(jnp.max(sh, axis=-1, keepdims=True), sink)
            e = jnp.exp(sh - mx)
            den = jnp.sum(e, axis=-1, keepdims=True) + jnp.exp(sink - mx)
            es.append(e.astype(BF16))
            invs.append(1.0 / den)
        return jnp.concatenate(es, axis=1), jnp.where(lane_low, invs[0], invs[1])

    def weighted_values(c, m, e, inv):
        vbd = block_diag(v_parts, c, (2 * m) // Q_PER_KV)
        return _dot(e, vbd) * inv

    its = [(c, m) for c in range(nblk) for m in range(N_HEADS // 2)]
    s_queue = [scores(*its[i]) for i in range(min(ATTN_LOOKAHEAD, len(its)))]
    o_parts = {}
    issued = PIECES_BEFORE_ATTN
    for i, (c, m) in enumerate(its):
        if i + ATTN_LOOKAHEAD < len(its):
            s_queue.append(scores(*its[i + ATTN_LOOKAHEAD]))
        e, inv = softmax_numerators(c, m, s_queue.pop(0))
        o_parts[(c, m)] = weighted_values(c, m, e, inv)
        while issued < PIECES_BEFORE_ATTN + ((i + 1) * (n_pieces - PIECES_BEFORE_ATTN)) // len(its):
            next(pieces)
            issued += 1
    o = jnp.concatenate(
        [jnp.concatenate([o_parts[(c, m)] for m in range(N_HEADS // 2)], axis=1)
         for c in range(nblk)], axis=0).astype(BF16)
    y_c = _dot(o, wc_ref[...])
    merged = jnp.concatenate(merged_ab, axis=1) + jnp.concatenate(gate_c, axis=1) * y_c

    merged = merged.astype(BF16)
    for c in range(nblk):
        rows = slice(c * BLK, (c + 1) * BLK)
        x1_ref[rows, :] = x[rows, :] + _rmsnorm(_dot(merged[rows, :], wo_ref[...]), gpost_ref[...])


def _prompt_mixer(layer, x, rope, sinks, gpre, gpost, w_in, lng, lnb, wsp, bsp, convw, convb,
                  wa, wb, wc, wo, cast_layer, cast_weights):
    n = x.shape[0]
    n_steps = n // TM_MIX
    lb = functools.partial(_layer_block, layer)
    casts = [_cast_specs(cast_layer, w, n_steps) for w in cast_weights]
    in_specs = [
        pl.BlockSpec((TM_MIX, D_MODEL), lambda i, *_: (i, 0)),
        pl.BlockSpec((None, 1, LANES), lambda i, *_: (i, 0, 0)),
        pl.BlockSpec((None, 1, LANES), lambda i, *_: (i, 0, 0)),
        _whole((TM_MIX, LANES)), _whole((TM_MIX, LANES)),
        _whole((TM_MIX, LANES)), _whole((TM_MIX, LANES)),
        lb((1, D_MODEL)), lb((1, D_MODEL)),
        _whole((D_MODEL, IN_WIDTH)),
        lb((1, D_MODEL)), lb((1, D_MODEL)),
        lb((A_GROUPS, CHUNK, CHUNK)), lb((CHUNK, D_MODEL)),
        lb((CONV_W, D_MODEL)), lb((1, D_MODEL)),
        _whole((D_MODEL, D_MODEL)), _whole((D_MODEL, D_MODEL)),
        _whole((D_MODEL, D_MODEL)), _whole((D_MODEL, D_MODEL)),
    ] + [c[0] for c in casts]
    assert len(in_specs) == N_MIXER_INPUTS + len(casts)
    out_specs = [
        pl.BlockSpec((TM_MIX, D_MODEL), lambda i, *_: (i, 0)),
        pl.BlockSpec((SUBLANES, D_MODEL), lambda i, *_: (0, 0)),
        pl.BlockSpec((BLK, KV_DIM), lambda i, *_: (0, 0)),
        pl.BlockSpec((BLK, KV_DIM), lambda i, *_: (0, 0)),
    ] + [c[1] for c in casts]
    out_shape = [
        jax.ShapeDtypeStruct((n, D_MODEL), F32),
        jax.ShapeDtypeStruct((SUBLANES, D_MODEL), F32),
        jax.ShapeDtypeStruct((BLK, KV_DIM), F32),
        jax.ShapeDtypeStruct((BLK, KV_DIM), F32),
    ] + [c[2] for c in casts]
    scratch = [
        pltpu.VMEM((A_GROUPS, CHUNK, CHUNK), BF16),
        pltpu.VMEM((N_KV_HEADS, 2, BLK, LANES), BF16),
        pltpu.VMEM((N_KV_HEADS, 2, BLK, LANES), BF16),
        pltpu.VMEM((SUBLANES, D_MODEL), F32),
    ]
    return pl.pallas_call(
        functools.partial(_mixer_kernel, layer, len(casts)),
        grid_spec=pltpu.PrefetchScalarGridSpec(
            num_scalar_prefetch=1, grid=(n_steps,), in_specs=in_specs, out_specs=out_specs,
            scratch_shapes=scratch),
        out_shape=out_shape,
        compiler_params=pltpu.CompilerParams(
            dimension_semantics=("arbitrary",), vmem_limit_bytes=V7X_VMEM_LIMIT),
        name="prompt_mixer",
    )(sinks, x, *rope, gpre, gpost, w_in, lng, lnb, wsp, bsp, convw, convb, wa, wb, wc, wo,
      *cast_weights)


def _swiglu(x, gpre, gpost, wg_ref, wu_ref, wd_ref):
    h = _rmsnorm(x, gpre).astype(BF16)
    a = jax.nn.silu(_dot(h, wg_ref[...])) * _dot(h, wu_ref[...])
    return x + _rmsnorm(_dot(a.astype(BF16), wd_ref[...]), gpost)


N_FFN_INPUTS = 6


def _ffn_kernel(n_cast, *refs):
    x_ref, gpre_ref, gpost_ref, wg_ref, wu_ref, wd_ref = refs[:N_FFN_INPUTS]
    cast_src = refs[N_FFN_INPUTS:N_FFN_INPUTS + n_cast]
    y_ref = refs[N_FFN_INPUTS + n_cast]
    cast_dst = refs[N_FFN_INPUTS + n_cast + 1:]
    rows = [slice(r * FFN_ROWS, (r + 1) * FFN_ROWS) for r in range(TM_FFN // FFN_ROWS)]
    xs = [x_ref[r, :] for r in rows]
    acts = []
    for x in xs:
        h = _rmsnorm(x, gpre_ref[...]).astype(BF16)
        acts.append((jax.nn.silu(_dot(h, wg_ref[...])) * _dot(h, wu_ref[...])).astype(BF16))
    for r, x, a in zip(rows, xs, acts):
        y_ref[r, :] = x + _rmsnorm(_dot(a, wd_ref[...]), gpost_ref[...])
    _cast_row_blocks(cast_src, cast_dst)


def _prompt_ffn(layer, x, gpre, gpost, wg, wu, wd, cast_layer, cast_weights):
    n = x.shape[0]
    n_steps = n // TM_FFN
    lb = functools.partial(_layer_block, layer)
    casts = [_cast_specs(cast_layer, w, n_steps) for w in cast_weights]
    row_tile = pl.BlockSpec((TM_FFN, D_MODEL), lambda i: (i, 0))
    return pl.pallas_call(
        functools.partial(_ffn_kernel, len(casts)),
        grid=(n_steps,),
        in_specs=[row_tile, lb((1, D_MODEL)), lb((1, D_MODEL)),
                  _whole((D_MODEL, D_FF)), _whole((D_MODEL, D_FF)), _whole((D_FF, D_MODEL))]
                 + [c[0] for c in casts],
        out_specs=[row_tile] + [c[1] for c in casts],
        out_shape=[jax.ShapeDtypeStruct((n, D_MODEL), F32)] + [c[2] for c in casts],
        compiler_params=pltpu.CompilerParams(
            dimension_semantics=("arbitrary",), vmem_limit_bytes=V7X_VMEM_LIMIT),
        name="prompt_ffn",
    )(x, gpre, gpost, wg, wu, wd, *cast_weights)


def _sample_proj_kernel(x_ref, hist_ref, cos_ref, sin_ref, gpre_ref, w_in_ref, lng_ref, lnb_ref,
                        wsp0_ref, bsp0_ref, convw_ref, convb_ref, wa_ref, wb_ref,
                        va_ref, newconv_ref, q_ref, k_ref, v_ref, mab_ref, gc_ref):
    x = x_ref[...]
    h = _rmsnorm(x, gpre_ref[...]).astype(BF16)

    def proj(off, width):
        return _dot(h, w_in_ref[:, off:off + width])

    ua = jax.nn.gelu(proj(OFF_UA, D_MODEL))
    va = _layernorm(jax.nn.gelu(proj(OFF_VA, D_MODEL)), lng_ref[...], lnb_ref[...])
    va_ref[...] = va
    sa = wsp0_ref[...] * va + bsp0_ref[...]
    y_a = _dot((ua * sa).astype(BF16), wa_ref[...])
    merged = jax.nn.sigmoid(proj(OFF_GA, D_MODEL)) * y_a

    cx = proj(OFF_CG, D_MODEL) * proj(OFF_HB, D_MODEL)
    h0 = hist_ref[:, 0:D_MODEL]
    h1 = hist_ref[:, D_MODEL:2 * D_MODEL]
    conv = convb_ref[...] + convw_ref[0:1, :] * h0
    conv = conv + convw_ref[1:2, :] * h1
    conv = conv + convw_ref[2:3, :] * cx
    newconv_ref[:, 0:D_MODEL] = h1
    newconv_ref[:, D_MODEL:2 * D_MODEL] = cx
    y_b = _dot((proj(OFF_BG, D_MODEL) * conv).astype(BF16), wb_ref[...])
    mab_ref[...] = merged + jax.nn.sigmoid(proj(OFF_GB, D_MODEL)) * y_b
    gc_ref[...] = jax.nn.sigmoid(proj(OFF_GC, D_MODEL))

    cos_t = cos_ref[...]
    sin_s = sin_ref[...]
    q_ref[...] = _rope(proj(OFF_Q, N_HEADS * HEAD_DIM), cos_t, sin_s) * (HEAD_DIM ** -0.5)
    k_ref[...] = _rope(proj(OFF_K, KV_DIM), cos_t, sin_s)
    v_ref[...] = proj(OFF_V, KV_DIM)


def _sample_proj(layer, x, hist, cos_t, sin_s, gpre, w_in, lng, lnb, wsp0, bsp0, convw, convb,
                 wa, wb):
    b = x.shape[0]
    lb = functools.partial(_layer_block, layer)
    in_specs = [
        _whole(x.shape), lb((b, (CONV_W - 1) * D_MODEL)), _whole(cos_t.shape), _whole(sin_s.shape),
        lb((1, D_MODEL)), _whole((D_MODEL, IN_WIDTH)), lb((1, D_MODEL)), lb((1, D_MODEL)),
        lb((1, D_MODEL)), lb((1, D_MODEL)), lb((CONV_W, D_MODEL)), lb((1, D_MODEL)),
        _whole((D_MODEL, D_MODEL)), _whole((D_MODEL, D_MODEL)),
    ]
    out_shape = [
        jax.ShapeDtypeStruct((b, D_MODEL), F32),
        jax.ShapeDtypeStruct((b, 2 * D_MODEL), F32),
        jax.ShapeDtypeStruct((b, D_MODEL), F32),
        jax.ShapeDtypeStruct((b, KV_DIM), F32),
        jax.ShapeDtypeStruct((b, KV_DIM), F32),
        jax.ShapeDtypeStruct((b, D_MODEL), F32),
        jax.ShapeDtypeStruct((b, D_MODEL), F32),
    ]
    return pl.pallas_call(
        _sample_proj_kernel,
        grid=(1,),
        in_specs=in_specs,
        out_specs=[pl.BlockSpec(s.shape, lambda i: (0, 0)) for s in out_shape],
        out_shape=out_shape,
        compiler_params=pltpu.CompilerParams(
            dimension_semantics=("arbitrary",), vmem_limit_bytes=V7X_VMEM_LIMIT),
        name="sample_proj",
    )(x, hist, cos_t, sin_s, gpre, w_in, lng, lnb, wsp0, bsp0, convw, convb, wa, wb)


def _sample_attn_kernel(layer, n_aliased, sinks_ref, *refs):
    (q_ref, knew_ref, vnew_ref, knew_t_ref, vnew_t_ref, kc_ref, vc_ref,
     o_ref, kout_ref, vout_ref) = refs[n_aliased:]
    step = pl.program_id(0)
    qf = q_ref[...]
    s = jnp.einsum('grd,gdk->grk', qf.astype(BF16), kc_ref[...].astype(BF16),
                   preferred_element_type=F32)
    s_new = jnp.sum(qf * knew_ref[...], axis=-1, keepdims=True)
    gidx = jax.lax.broadcasted_iota(jnp.int32, (S2_GROUPS, Q_PER_KV, 1), 0)
    ridx = jax.lax.broadcasted_iota(jnp.int32, (S2_GROUPS, Q_PER_KV, 1), 1)
    head = (gidx % N_KV_HEADS) * Q_PER_KV + ridx
    sink = jnp.zeros((S2_GROUPS, Q_PER_KV, 1), F32)
    for hh in range(N_HEADS):
        sink = jnp.where(head == hh, sinks_ref[layer, hh], sink)
    mx = jnp.maximum(jnp.maximum(jnp.max(s, axis=-1, keepdims=True), s_new), sink)
    e = jnp.exp(s - mx)
    e_new = jnp.exp(s_new - mx)
    den = jnp.sum(e, axis=-1, keepdims=True) + e_new + jnp.exp(sink - mx)
    o = jnp.einsum('grk,gdk->grd', e.astype(BF16), vc_ref[...].astype(BF16),
                   preferred_element_type=F32)
    o_ref[...] = (o + e_new * vnew_ref[...]) * (1.0 / den)

    steps_per_tile = LANES // S2_GROUPS
    tile = pl.multiple_of((step // steps_per_tile) * LANES, LANES)
    first_col = (step % steps_per_tile) * S2_GROUPS
    last_lane = _lane_iota((HEAD_DIM, WINDOW)) == WINDOW - 1
    for new_t_ref, cache_ref, out_ref in ((knew_t_ref, kc_ref, kout_ref),
                                          (vnew_t_ref, vc_ref, vout_ref)):
        new_cols = new_t_ref[:, pl.ds(tile, LANES)]
        for j in range(S2_GROUPS):
            col = pltpu.roll(new_cols, WINDOW - 1 - first_col - j, axis=1)
            out_ref[j] = jnp.where(last_lane, col, pltpu.roll(cache_ref[j], WINDOW - 1, axis=1))


def _sample_attn(layer, sinks, q, knew, vnew, knew_t, vnew_t, kcache, vcache, kout_prev, vout_prev):
    groups = q.shape[0]
    group3 = lambda d1, d2: pl.BlockSpec((S2_GROUPS, d1, d2), lambda i, *_: (i, 0, 0))
    cache_blk = pl.BlockSpec((None, S2_GROUPS, HEAD_DIM, WINDOW), lambda i, *_: (layer, i, 0, 0))
    aliased = [] if kout_prev is None else [kout_prev, vout_prev]
    n_al = len(aliased)
    in_specs = ([pl.BlockSpec(memory_space=pl.ANY)] * n_al +
                [group3(Q_PER_KV, HEAD_DIM), group3(1, HEAD_DIM), group3(1, HEAD_DIM),
                 _whole(knew_t.shape), _whole(vnew_t.shape), cache_blk, cache_blk])
    io_alias = {1: 1, 2: 2} if n_al else {}
    return pl.pallas_call(
        functools.partial(_sample_attn_kernel, layer, n_al),
        grid_spec=pltpu.PrefetchScalarGridSpec(
            num_scalar_prefetch=1, grid=(groups // S2_GROUPS,),
            in_specs=in_specs,
            out_specs=[group3(Q_PER_KV, HEAD_DIM), cache_blk, cache_blk]),
        out_shape=[jax.ShapeDtypeStruct((groups, Q_PER_KV, HEAD_DIM), F32),
                   jax.ShapeDtypeStruct(kcache.shape, F32),
                   jax.ShapeDtypeStruct(vcache.shape, F32)],
        input_output_aliases=io_alias,
        compiler_params=pltpu.CompilerParams(
            dimension_semantics=("arbitrary",), vmem_limit_bytes=V7X_VMEM_LIMIT),
        name="sample_attn",
    )(sinks, *aliased, q, knew, vnew, knew_t, vnew_t, kcache, vcache)


def _sample_out_kernel(x_ref, o_ref, mab_ref, gc_ref, gpost_ref, gpre2_ref, gpost2_ref,
                       wc_ref, wo_ref, wg_ref, wu_ref, wd_ref, y_ref):
    y_c = _dot(o_ref[...].astype(BF16), wc_ref[...])
    merged = mab_ref[...] + gc_ref[...] * y_c
    x1 = x_ref[...] + _rmsnorm(_dot(merged.astype(BF16), wo_ref[...]), gpost_ref[...])
    y_ref[...] = _swiglu(x1, gpre2_ref[...], gpost2_ref[...], wg_ref, wu_ref, wd_ref)


def _sample_out(layer, x, o, mab, gc, gpost, gpre2, gpost2, wc, wo, wg, wu, wd):
    lb = functools.partial(_layer_block, layer)
    in_specs = [_whole(x.shape), _whole(o.shape), _whole(mab.shape), _whole(gc.shape),
                lb((1, D_MODEL)), lb((1, D_MODEL)), lb((1, D_MODEL)),
                _whole((D_MODEL, D_MODEL)), _whole((D_MODEL, D_MODEL)),
                _whole((D_MODEL, D_FF)), _whole((D_MODEL, D_FF)), _whole((D_FF, D_MODEL))]
    return pl.pallas_call(
        _sample_out_kernel,
        grid=(1,),
        in_specs=in_specs,
        out_specs=pl.BlockSpec(x.shape, lambda i: (0, 0)),
        out_shape=jax.ShapeDtypeStruct(x.shape, F32),
        compiler_params=pltpu.CompilerParams(
            dimension_semantics=("arbitrary",), vmem_limit_bytes=V7X_VMEM_LIMIT),
        name="sample_out",
    )(x, o, mab, gc, gpost, gpre2, gpost2, wc, wo, wg, wu, wd)


def _rope_angles(pos):
    half = HEAD_DIM // 2
    inv = jnp.power(jnp.float32(ROPE_THETA), -jnp.arange(half, dtype=F32) * (2.0 / HEAD_DIM))
    return pos.astype(F32)[:, None] * jnp.tile(inv, LANES // half)[None, :]


def _rope_sign():
    lane = jnp.arange(LANES, dtype=jnp.int32)
    return jnp.where((lane % HEAD_DIM) < (HEAD_DIM // 2), -1.0, 1.0).astype(F32)[None, :]


def kernel(x_prompt, x_sample, state_conv, cache_win_k, cache_win_v, norm_pre_mix, norm_post_mix,
           norm_pre_ffn, norm_post_ffn, w_in, chunk_ln_g, chunk_ln_b, w_spatial, b_spatial, conv_w,
           conv_b, attn_sinks, w_br_a, w_br_b, w_br_c, w_out, w_ffn_gate, w_ffn_up, w_ffn_down):
    xp = x_prompt.reshape(SEQ, D_MODEL)
    xs = x_sample.reshape(DEC_BATCH, D_MODEL)

    sign = _rope_sign()
    ang_a = _rope_angles(jnp.arange(SEQ // TM_MIX, dtype=jnp.int32) * TM_MIX)
    ang_b = _rope_angles(jnp.arange(TM_MIX, dtype=jnp.int32))
    rope_p = (jnp.cos(ang_a)[:, None, :], jnp.sin(ang_a)[:, None, :],
              jnp.cos(ang_b), jnp.sin(ang_b), sign * jnp.cos(ang_b), sign * jnp.sin(ang_b))
    ang_s = _rope_angles(jnp.full((1,), PAST_LEN, dtype=jnp.int32))
    cos_s, sin_s = jnp.cos(ang_s), sign * jnp.sin(ang_s)

    mixer_f32 = (w_in, w_br_a, w_br_b, w_br_c, w_out)
    ffn_f32 = (w_ffn_gate, w_ffn_up, w_ffn_down)
    mixer_w = tuple(w[0].astype(BF16) for w in mixer_f32)
    rows = lambda a: a.reshape(DEPTH, 1, -1)
    gpre, gpost = rows(norm_pre_mix), rows(norm_post_mix)
    gpre2, gpost2 = rows(norm_pre_ffn), rows(norm_post_ffn)
    lng, lnb, convb = rows(chunk_ln_g), rows(chunk_ln_b), rows(conv_b)
    bsp = jnp.repeat(jnp.transpose(b_spatial, (0, 2, 1)), A_GROUP_DIM, axis=2)
    wsp0 = rows(jnp.repeat(w_spatial[:, :, 0, 0], A_GROUP_DIM, axis=1))
    bsp0 = rows(jnp.repeat(b_spatial[:, :, 0], A_GROUP_DIM, axis=1))
    hist = state_conv.reshape(DEPTH, DEC_BATCH, (CONV_W - 1) * D_MODEL)
    n_groups = DEC_BATCH * N_KV_HEADS
    to_groups = lambda c: jnp.transpose(c, (0, 1, 3, 4, 2)).reshape(DEPTH, n_groups, HEAD_DIM, WINDOW)
    kcache, vcache = to_groups(cache_win_k), to_groups(cache_win_v)

    p_conv, p_k, p_v = [], [], []
    s_conv, s_cv = [], []
    knext = vnext = None
    for l in range(DEPTH):
        w_in_b, wa, wb, wc, wo = mixer_w
        x1, cxlast, klast, vlast, wg, wu, wd = _prompt_mixer(
            l, xp, rope_p, attn_sinks, gpre, gpost, w_in_b, lng, lnb, w_spatial, bsp,
            conv_w, convb, wa, wb, wc, wo, l, ffn_f32)
        if l + 1 < DEPTH:
            xp, *mixer_w = _prompt_ffn(l, x1, gpre2, gpost2, wg, wu, wd, l + 1, mixer_f32)
        else:
            (xp,) = _prompt_ffn(l, x1, gpre2, gpost2, wg, wu, wd, 0, ())
        p_conv.append(cxlast[SUBLANES - (CONV_W - 1):].reshape(1, CONV_W - 1, D_MODEL))
        p_k.append(klast.reshape(1, WINDOW, N_KV_HEADS, HEAD_DIM))
        p_v.append(vlast.reshape(1, WINDOW, N_KV_HEADS, HEAD_DIM))

        va, newconv, q, knew, vnew, mab, gc = _sample_proj(
            l, xs, hist, cos_s, sin_s, gpre, w_in_b, lng, lnb, wsp0, bsp0, conv_w, convb, wa, wb)
        knew = knew.reshape(n_groups, HEAD_DIM)
        vnew = vnew.reshape(n_groups, HEAD_DIM)
        o, knext, vnext = _sample_attn(
            l, attn_sinks, q.reshape(n_groups, Q_PER_KV, HEAD_DIM),
            knew[:, None, :], vnew[:, None, :], jnp.transpose(knew), jnp.transpose(vnew),
            kcache, vcache, knext, vnext)
        xs = _sample_out(l, xs, o.reshape(DEC_BATCH, D_MODEL), mab, gc, gpost, gpre2, gpost2,
                         wc, wo, wg, wu, wd)
        s_conv.append(newconv.reshape(DEC_BATCH, CONV_W - 1, D_MODEL))
        s_cv.append(va.reshape(DEC_BATCH, 1, D_MODEL))

    from_groups = lambda c: jnp.transpose(
        c.reshape(DEPTH, DEC_BATCH, N_KV_HEADS, HEAD_DIM, WINDOW), (0, 1, 4, 2, 3))
    return (xp.reshape(1, SEQ, D_MODEL), xs.reshape(DEC_BATCH, 1, D_MODEL),
            jnp.stack(p_conv), jnp.stack(p_k), jnp.stack(p_v),
            jnp.stack(s_conv), from_groups(knext), from_groups(vnext), jnp.stack(s_cv))
```

```python
import functools

import jax
import jax.numpy as jnp
from jax.experimental import pallas as pl
from jax.experimental.pallas import tpu as pltpu

D_MODEL = 1024
SEQ = 16384
DEPTH = 2
DEC_BATCH = 128
PAST_LEN = 16384
CHUNK = 128
A_GROUPS = 8
A_GROUP_DIM = D_MODEL // A_GROUPS
CONV_W = 3
N_HEADS = 16
N_KV_HEADS = 4
HEAD_DIM = 64
Q_PER_KV = N_HEADS // N_KV_HEADS
WINDOW = 128
ROPE_THETA = 10000.0
D_FF = 2816
KV_DIM = N_KV_HEADS * HEAD_DIM
NEG_INF = -1e30

OFF_UA = 0
OFF_VA = OFF_UA + D_MODEL
OFF_BG = OFF_VA + D_MODEL
OFF_CG = OFF_BG + D_MODEL
OFF_HB = OFF_CG + D_MODEL
OFF_Q = OFF_HB + D_MODEL
OFF_K = OFF_Q + N_HEADS * HEAD_DIM
OFF_V = OFF_K + KV_DIM
OFF_GA = OFF_V + KV_DIM
OFF_GB = OFF_GA + D_MODEL
OFF_GC = OFF_GB + D_MODEL
IN_WIDTH = OFF_GC + D_MODEL

LANES = 128
SUBLANES = 8
BF16_SUBLANES = 16
V7X_VMEM_LIMIT = 62 * 1024 * 1024

TM_MIX = 512
TM_FFN = 512
FFN_ROWS = 256
BLK = WINDOW
S2_GROUPS = 64
ATTN_LOOKAHEAD = 2
GATE_COLS = 256
PIECES_BEFORE_ATTN = 3

assert CHUNK == BLK and WINDOW == BLK and PAST_LEN >= WINDOW
assert 2 * HEAD_DIM == LANES and A_GROUP_DIM == LANES
assert SEQ % TM_MIX == 0 and SEQ % TM_FFN == 0 and TM_MIX % BLK == 0
assert LANES % S2_GROUPS == 0 and S2_GROUPS % N_KV_HEADS == 0 and WINDOW == LANES
assert (DEC_BATCH * N_KV_HEADS) % LANES == 0

BF16 = jnp.bfloat16
F32 = jnp.float32


def _dot(a, b):
    return jnp.dot(a, b, preferred_element_type=F32)


def _dot_nt(a, b):
    return jax.lax.dot_general(a, b, (((1,), (1,)), ((), ())), preferred_element_type=F32)


def _rmsnorm(x, g, eps=1e-6):
    return x * jax.lax.rsqrt(jnp.mean(x * x, axis=-1, keepdims=True) + eps) * g


def _layernorm(x, g, b, eps=1e-5):
    mu = jnp.mean(x, axis=-1, keepdims=True)
    xc = x - mu
    var = jnp.mean(xc * xc, axis=-1, keepdims=True)
    return xc * jax.lax.rsqrt(var + eps) * g + b


def _lane_iota(shape):
    return jax.lax.broadcasted_iota(jnp.int32, shape, len(shape) - 1)


def _rope_block(xb, cos_t, sin_s):
    lane = _lane_iota(xb.shape)
    first_half = (lane % HEAD_DIM) < (HEAD_DIM // 2)
    rot = jnp.where(first_half,
                    pltpu.roll(xb, LANES - HEAD_DIM // 2, axis=1),
                    pltpu.roll(xb, HEAD_DIM // 2, axis=1))
    return xb * cos_t + rot * sin_s


def _rope(x, cos_t, sin_s):
    n = x.shape[1] // LANES
    return jnp.concatenate(
        [_rope_block(x[:, j * LANES:(j + 1) * LANES], cos_t, sin_s) for j in range(n)], axis=1)


def _split_kv_heads(blk):
    out = []
    for j in range(KV_DIM // LANES):
        b = blk[:, j * LANES:(j + 1) * LANES]
        r = pltpu.roll(b, HEAD_DIM, axis=1)
        low = _lane_iota(b.shape) < HEAD_DIM
        zero = jnp.zeros_like(b)
        out.append((jnp.where(low, b, zero).astype(BF16), jnp.where(low, zero, r).astype(BF16)))
        out.append((jnp.where(low, r, zero).astype(BF16), jnp.where(low, zero, b).astype(BF16)))
    return out


def _layer_block(layer, shape, single_buffer=True):
    nd = len(shape)
    mode = pl.Buffered(1) if single_buffer else None
    return pl.BlockSpec((None,) + tuple(shape), lambda *_: (layer,) + (0,) * nd, pipeline_mode=mode)


def _whole(shape):
    nd = len(shape)
    return pl.BlockSpec(tuple(shape), lambda *_: (0,) * nd, pipeline_mode=pl.Buffered(1))


N_MIXER_INPUTS = 20


def _cast_plan(rows, n_steps):
    for nb in range(n_steps, 0, -1):
        if rows % nb == 0 and (rows // nb) % BF16_SUBLANES == 0:
            return nb
    raise ValueError(f"no bf16-aligned row split of {rows} rows over {n_steps} steps")


def _cast_specs(layer, w_stacked, n_steps):
    _, rows, cols = w_stacked.shape
    nb = _cast_plan(rows, n_steps)
    blk = lambda i: jnp.minimum(i, nb - 1)
    return (pl.BlockSpec((None, rows // nb, cols), lambda i, *_: (layer, blk(i), 0)),
            pl.BlockSpec((rows // nb, cols), lambda i, *_: (blk(i), 0)),
            jax.ShapeDtypeStruct((rows, cols), BF16))


def _cast_row_blocks(src_refs, dst_refs):
    for src, dst in zip(src_refs, dst_refs):
        dst[...] = src[...].astype(BF16)


def _mixer_kernel(layer, n_cast, sinks_ref, *refs):
    (x_ref, cos_a_ref, sin_a_ref, cos_b_ref, sin_b_ref, cos_bs_ref, sin_bs_ref,
     gpre_ref, gpost_ref, w_in_ref, lng_ref, lnb_ref, wsp_ref, bsp_ref, convw_ref, convb_ref,
     wa_ref, wb_ref, wc_ref, wo_ref) = refs[:N_MIXER_INPUTS]
    cast_src = refs[N_MIXER_INPUTS:N_MIXER_INPUTS + n_cast]
    outs = refs[N_MIXER_INPUTS + n_cast:]
    x1_ref, cxlast_ref, klast_ref, vlast_ref = outs[:4]
    cast_dst = outs[4:4 + n_cast]
    wsp_s, kprev_s, vprev_s, cxprev_s = outs[4 + n_cast:]
    step = pl.program_id(0)
    nblk = TM_MIX // BLK

    @pl.when(step == 0)
    def _init():
        row = jax.lax.broadcasted_iota(jnp.int32, (CHUNK, CHUNK), 0)
        col = jax.lax.broadcasted_iota(jnp.int32, (CHUNK, CHUNK), 1)
        for g in range(A_GROUPS):
            wsp_s[g] = jnp.where(col <= row, wsp_ref[g], 0.0).astype(BF16)
        kprev_s[...] = jnp.zeros_like(kprev_s)
        vprev_s[...] = jnp.zeros_like(vprev_s)
        cxprev_s[...] = jnp.zeros_like(cxprev_s)

    x = x_ref[...]
    h = _rmsnorm(x, gpre_ref[...]).astype(BF16)

    def proj(off, width):
        return _dot(h, w_in_ref[:, off:off + width])

    va = _layernorm(jax.nn.gelu(proj(OFF_VA, D_MODEL)), lng_ref[...], lnb_ref[...]).astype(BF16)
    ua = jax.nn.gelu(proj(OFF_UA, D_MODEL))
    cx = proj(OFF_CG, D_MODEL) * proj(OFF_HB, D_MODEL)
    _cast_row_blocks(cast_src, cast_dst)
    prev = cxprev_s[...]
    rowi = jax.lax.broadcasted_iota(jnp.int32, cx.shape, 0)
    cx1 = jnp.where(rowi < 1, prev[SUBLANES - 1:SUBLANES, :], pltpu.roll(cx, 1, axis=0))
    cx2 = jnp.where(rowi < 2,
                    jnp.where(rowi < 1, prev[SUBLANES - 2:SUBLANES - 1, :],
                              prev[SUBLANES - 1:SUBLANES, :]),
                    pltpu.roll(cx, 2, axis=0))
    conv = convb_ref[...] + convw_ref[0:1, :] * cx2
    conv = conv + convw_ref[1:2, :] * cx1
    conv = conv + convw_ref[2:3, :] * cx
    cxprev_s[...] = cx[TM_MIX - SUBLANES:, :]
    cxlast_ref[...] = cx[TM_MIX - SUBLANES:, :]
    bx = (proj(OFF_BG, D_MODEL) * conv).astype(BF16)
    sa_rows = []
    for c in range(nblk):
        sa_rows.append(jnp.concatenate(
            [_dot(wsp_s[g], va[c * BLK:(c + 1) * BLK, g * LANES:(g + 1) * LANES])
             for g in range(A_GROUPS)], axis=1) + bsp_ref[...])
    ax = (ua * jnp.concatenate(sa_rows, axis=0)).astype(BF16)

    cos_a, sin_a = cos_a_ref[...], sin_a_ref[...]
    cos_t = cos_a * cos_b_ref[...] - sin_a * sin_b_ref[...]
    sin_s = sin_a * cos_bs_ref[...] + cos_a * sin_bs_ref[...]
    q = (_rope(proj(OFF_Q, N_HEADS * HEAD_DIM), cos_t, sin_s) * (HEAD_DIM ** -0.5)).astype(BF16)
    k = _rope(proj(OFF_K, KV_DIM), cos_t, sin_s)
    v = proj(OFF_V, KV_DIM)
    klast_ref[...] = k[TM_MIX - BLK:, :]
    vlast_ref[...] = v[TM_MIX - BLK:, :]

    merged_ab, gate_c = [], []

    def gated_branch_pieces():
        for j in range(D_MODEL // GATE_COLS):
            cols = slice(j * GATE_COLS, (j + 1) * GATE_COLS)
            y_a = _dot(ax, wa_ref[:, cols])
            yield
            part = jax.nn.sigmoid(proj(OFF_GA + j * GATE_COLS, GATE_COLS)) * y_a
            yield
            y_b = _dot(bx, wb_ref[:, cols])
            yield
            merged_ab.append(part + jax.nn.sigmoid(proj(OFF_GB + j * GATE_COLS, GATE_COLS)) * y_b)
            yield
            gate_c.append(jax.nn.sigmoid(proj(OFF_GC + j * GATE_COLS, GATE_COLS)))
            yield

    pieces = gated_branch_pieces()
    n_pieces = 5 * (D_MODEL // GATE_COLS)
    for _ in range(PIECES_BEFORE_ATTN):
        next(pieces)

    qi = jax.lax.broadcasted_iota(jnp.int32, (BLK, 2 * BLK), 0)
    kj = jax.lax.broadcasted_iota(jnp.int32, (BLK, 2 * BLK), 1)
    band = (kj >= qi) & (kj <= qi + WINDOW)
    lane_low = _lane_iota((BLK, LANES)) < HEAD_DIM

    k_parts = [[(kprev_s[g, 0], kprev_s[g, 1]) for g in range(N_KV_HEADS)]]
    v_parts = [[(vprev_s[g, 0], vprev_s[g, 1]) for g in range(N_KV_HEADS)]]
    for c in range(nblk):
        k_parts.append(_split_kv_heads(k[c * BLK:(c + 1) * BLK, :]))
        v_parts.append(_split_kv_heads(v[c * BLK:(c + 1) * BLK, :]))
    for g in range(N_KV_HEADS):
        kprev_s[g, 0] = k_parts[nblk][g][0]
        kprev_s[g, 1] = k_parts[nblk][g][1]
        vprev_s[g, 0] = v_parts[nblk][g][0]
        vprev_s[g, 1] = v_parts[nblk][g][1]
    mask_first = band & (kj >= jnp.where(step == 0, BLK, 0))

    def block_diag(parts, c, g):
        return jnp.concatenate([parts[c][g][0], parts[c + 1][g][0],
                                parts[c][g][1], parts[c + 1][g][1]], axis=0)

    def scores(c, m):
        kbd = block_diag(k_parts, c, (2 * m) // Q_PER_KV)
        return _dot_nt(q[c * BLK:(c + 1) * BLK, m * LANES:(m + 1) * LANES], kbd)

    def softmax_numerators(c, m, s):
        mask = mask_first if c == 0 else band
        es, invs = [], []
        for hh in range(2):
            sink = sinks_ref[layer, 2 * m + hh]
            sh = jnp.where(mask, s[:, hh * 2 * BLK:(hh + 1) * 2 * BLK], NEG_INF)
            mx = jnp.maximum(jnp.max(sh, axis=-1, keepdims=True), sink)
            e = jnp.exp(sh - mx)
            den = jnp.sum(e, axis=-1, keepdims=True) + jnp.exp(sink - mx)
            es.append(e.astype(BF16))
            invs.append(1.0 / den)
        return jnp.concatenate(es, axis=1), jnp.where(lane_low, invs[0], invs[1])

    def weighted_values(c, m, e, inv):
        vbd = block_diag(v_parts, c, (2 * m) // Q_PER_KV)
        return _dot(e, vbd) * inv

    its = [(c, m) for c in range(nblk) for m in range(N_HEADS // 2)]
    s_queue = [scores(*its[i]) for i in range(min(ATTN_LOOKAHEAD, len(its)))]
    o_parts = {}
    issued = PIECES_BEFORE_ATTN
    for i, (c, m) in enumerate(its):
        if i + ATTN_LOOKAHEAD < len(its):
            s_queue.append(scores(*its[i + ATTN_LOOKAHEAD]))
        e, inv = softmax_numerators(c, m, s_queue.pop(0))
        o_parts[(c, m)] = weighted_values(c, m, e, inv)
        while issued < PIECES_BEFORE_ATTN + ((i + 1) * (n_pieces - PIECES_BEFORE_ATTN)) // len(its):
            next(pieces)
            issued += 1
    o = jnp.concatenate(
        [jnp.concatenate([o_parts[(c, m)] for m in range(N_HEADS // 2)], axis=1)
         for c in range(nblk)], axis=0).astype(BF16)
    y_c = _dot(o, wc_ref[...])
    merged = jnp.concatenate(merged_ab, axis=1) + jnp.concatenate(gate_c, axis=1) * y_c

    merged = merged.astype(BF16)
    for c in range(nblk):
        rows = slice(c * BLK, (c + 1) * BLK)
        x1_ref[rows, :] = x[rows, :] + _rmsnorm(_dot(merged[rows, :], wo_ref[...]), gpost_ref[...])


def _prompt_mixer(layer, x, rope, sinks, gpre, gpost, w_in, lng, lnb, wsp, bsp, convw, convb,
                  wa, wb, wc, wo, cast_layer, cast_weights):
    n = x.shape[0]
    n_steps = n // TM_MIX
    lb = functools.partial(_layer_block, layer)
    casts = [_cast_specs(cast_layer, w, n_steps) for w in cast_weights]
    in_specs = [
        pl.BlockSpec((TM_MIX, D_MODEL), lambda i, *_: (i, 0)),
        pl.BlockSpec((None, 1, LANES), lambda i, *_: (i, 0, 0)),
        pl.BlockSpec((None, 1, LANES), lambda i, *_: (i, 0, 0)),
        _whole((TM_MIX, LANES)), _whole((TM_MIX, LANES)),
        _whole((TM_MIX, LANES)), _whole((TM_MIX, LANES)),
        lb((1, D_MODEL)), lb((1, D_MODEL)),
        _whole((D_MODEL, IN_WIDTH)),
        lb((1, D_MODEL)), lb((1, D_MODEL)),
        lb((A_GROUPS, CHUNK, CHUNK)), lb((CHUNK, D_MODEL)),
        lb((CONV_W, D_MODEL)), lb((1, D_MODEL)),
        _whole((D_MODEL, D_MODEL)), _whole((D_MODEL, D_MODEL)),
        _whole((D_MODEL, D_MODEL)), _whole((D_MODEL, D_MODEL)),
    ] + [c[0] for c in casts]
    assert len(in_specs) == N_MIXER_INPUTS + len(casts)
    out_specs = [
        pl.BlockSpec((TM_MIX, D_MODEL), lambda i, *_: (i, 0)),
        pl.BlockSpec((SUBLANES, D_MODEL), lambda i, *_: (0, 0)),
        pl.BlockSpec((BLK, KV_DIM), lambda i, *_: (0, 0)),
        pl.BlockSpec((BLK, KV_DIM), lambda i, *_: (0, 0)),
    ] + [c[1] for c in casts]
    out_shape = [
        jax.ShapeDtypeStruct((n, D_MODEL), F32),
        jax.ShapeDtypeStruct((SUBLANES, D_MODEL), F32),
        jax.ShapeDtypeStruct((BLK, KV_DIM), F32),
        jax.ShapeDtypeStruct((BLK, KV_DIM), F32),
    ] + [c[2] for c in casts]
    scratch = [
        pltpu.VMEM((A_GROUPS, CHUNK, CHUNK), BF16),
        pltpu.VMEM((N_KV_HEADS, 2, BLK, LANES), BF16),
        pltpu.VMEM((N_KV_HEADS, 2, BLK, LANES), BF16),
        pltpu.VMEM((SUBLANES, D_MODEL), F32),
    ]
    return pl.pallas_call(
        functools.partial(_mixer_kernel, layer, len(casts)),
        grid_spec=pltpu.PrefetchScalarGridSpec(
            num_scalar_prefetch=1, grid=(n_steps,), in_specs=in_specs, out_specs=out_specs,
            scratch_shapes=scratch),
        out_shape=out_shape,
        compiler_params=pltpu.CompilerParams(
            dimension_semantics=("arbitrary",), vmem_limit_bytes=V7X_VMEM_LIMIT),
        name="prompt_mixer",
    )(sinks, x, *rope, gpre, gpost, w_in, lng, lnb, wsp, bsp, convw, convb, wa, wb, wc, wo,
      *cast_weights)


def _swiglu(x, gpre, gpost, wg_ref, wu_ref, wd_ref):
    h = _rmsnorm(x, gpre).astype(BF16)
    a = jax.nn.silu(_dot(h, wg_ref[...])) * _dot(h, wu_ref[...])
    return x + _rmsnorm(_dot(a.astype(BF16), wd_ref[...]), gpost)


N_FFN_INPUTS = 6


def _ffn_kernel(n_cast, *refs):
    x_ref, gpre_ref, gpost_ref, wg_ref, wu_ref, wd_ref = refs[:N_FFN_INPUTS]
    cast_src = refs[N_FFN_INPUTS:N_FFN_INPUTS + n_cast]
    y_ref = refs[N_FFN_INPUTS + n_cast]
    cast_dst = refs[N_FFN_INPUTS + n_cast + 1:]
    rows = [slice(r * FFN_ROWS, (r + 1) * FFN_ROWS) for r in range(TM_FFN // FFN_ROWS)]
    xs = [x_ref[r, :] for r in rows]
    acts = []
    for x in xs:
        h = _rmsnorm(x, gpre_ref[...]).astype(BF16)
        acts.append((jax.nn.silu(_dot(h, wg_ref[...])) * _dot(h, wu_ref[...])).astype(BF16))
    for r, x, a in zip(rows, xs, acts):
        y_ref[r, :] = x + _rmsnorm(_dot(a, wd_ref[...]), gpost_ref[...])
    _cast_row_blocks(cast_src, cast_dst)


def _prompt_ffn(layer, x, gpre, gpost, wg, wu, wd, cast_layer, cast_weights):
    n = x.shape[0]
    n_steps = n // TM_FFN
    lb = functools.partial(_layer_block, layer)
    casts = [_cast_specs(cast_layer, w, n_steps) for w in cast_weights]
    row_tile = pl.BlockSpec((TM_FFN, D_MODEL), lambda i: (i, 0))
    return pl.pallas_call(
        functools.partial(_ffn_kernel, len(casts)),
        grid=(n_steps,),
        in_specs=[row_tile, lb((1, D_MODEL)), lb((1, D_MODEL)),
                  _whole((D_MODEL, D_FF)), _whole((D_MODEL, D_FF)), _whole((D_FF, D_MODEL))]
                 + [c[0] for c in casts],
        out_specs=[row_tile] + [c[1] for c in casts],
        out_shape=[jax.ShapeDtypeStruct((n, D_MODEL), F32)] + [c[2] for c in casts],
        compiler_params=pltpu.CompilerParams(
            dimension_semantics=("arbitrary",), vmem_limit_bytes=V7X_VMEM_LIMIT),
        name="prompt_ffn",
    )(x, gpre, gpost, wg, wu, wd, *cast_weights)


def _sample_proj_kernel(x_ref, hist_ref, cos_ref, sin_ref, gpre_ref, w_in_ref, lng_ref, lnb_ref,
                        wsp0_ref, bsp0_ref, convw_ref, convb_ref, wa_ref, wb_ref,
                        va_ref, newconv_ref, q_ref, k_ref, v_ref, mab_ref, gc_ref):
    x = x_ref[...]
    h = _rmsnorm(x, gpre_ref[...]).astype(BF16)

    def proj(off, width):
        return _dot(h, w_in_ref[:, off:off + width])

    ua = jax.nn.gelu(proj(OFF_UA, D_MODEL))
    va = _layernorm(jax.nn.gelu(proj(OFF_VA, D_MODEL)), lng_ref[...], lnb_ref[...])
    va_ref[...] = va
    sa = wsp0_ref[...] * va + bsp0_ref[...]
    y_a = _dot((ua * sa).astype(BF16), wa_ref[...])
    merged = jax.nn.sigmoid(proj(OFF_GA, D_MODEL)) * y_a

    cx = proj(OFF_CG, D_MODEL) * proj(OFF_HB, D_MODEL)
    h0 = hist_ref[:, 0:D_MODEL]
    h1 = hist_ref[:, D_MODEL:2 * D_MODEL]
    conv = convb_ref[...] + convw_ref[0:1, :] * h0
    conv = conv + convw_ref[1:2, :] * h1
    conv = conv + convw_ref[2:3, :] * cx
    newconv_ref[:, 0:D_MODEL] = h1
    newconv_ref[:, D_MODEL:2 * D_MODEL] = cx
    y_b = _dot((proj(OFF_BG, D_MODEL) * conv).astype(BF16), wb_ref[...])
    mab_ref[...] = merged + jax.nn.sigmoid(proj(OFF_GB, D_MODEL)) * y_b
    gc_ref[...] = jax.nn.sigmoid(proj(OFF_GC, D_MODEL))

    cos_t = cos_ref[...]
    sin_s = sin_ref[...]
    q_ref[...] = _rope(proj(OFF_Q, N_HEADS * HEAD_DIM), cos_t, sin_s) * (HEAD_DIM ** -0.5)
    k_ref[...] = _rope(proj(OFF_K, KV_DIM), cos_t, sin_s)
    v_ref[...] = proj(OFF_V, KV_DIM)


def _sample_proj(layer, x, hist, cos_t, sin_s, gpre, w_in, lng, lnb, wsp0, bsp0, convw, convb,
                 wa, wb):
    b = x.shape[0]
    lb = functools.partial(_layer_block, layer)
    in_specs = [
        _whole(x.shape), lb((b, (CONV_W - 1) * D_MODEL)), _whole(cos_t.shape), _whole(sin_s.shape),
        lb((1, D_MODEL)), _whole((D_MODEL, IN_WIDTH)), lb((1, D_MODEL)), lb((1, D_MODEL)),
        lb((1, D_MODEL)), lb((1, D_MODEL)), lb((CONV_W, D_MODEL)), lb((1, D_MODEL)),
        _whole((D_MODEL, D_MODEL)), _whole((D_MODEL, D_MODEL)),
    ]
    out_shape = [
        jax.ShapeDtypeStruct((b, D_MODEL), F32),
        jax.ShapeDtypeStruct((b, 2 * D_MODEL), F32),
        jax.ShapeDtypeStruct((b, D_MODEL), F32),
        jax.ShapeDtypeStruct((b, KV_DIM), F32),
        jax.ShapeDtypeStruct((b, KV_DIM), F32),
        jax.ShapeDtypeStruct((b, D_MODEL), F32),
        jax.ShapeDtypeStruct((b, D_MODEL), F32),
    ]
    return pl.pallas_call(
        _sample_proj_kernel,
        grid=(1,),
        in_specs=in_specs,
        out_specs=[pl.BlockSpec(s.shape, lambda i: (0, 0)) for s in out_shape],
        out_shape=out_shape,
        compiler_params=pltpu.CompilerParams(
            dimension_semantics=("arbitrary",), vmem_limit_bytes=V7X_VMEM_LIMIT),
        name="sample_proj",
    )(x, hist, cos_t, sin_s, gpre, w_in, lng, lnb, wsp0, bsp0, convw, convb, wa, wb)


def _sample_attn_kernel(layer, n_aliased, sinks_ref, *refs):
    (q_ref, knew_ref, vnew_ref, knew_t_ref, vnew_t_ref, kc_ref, vc_ref,
     o_ref, kout_ref, vout_ref) = refs[n_aliased:]
    step = pl.program_id(0)
    qf = q_ref[...]
    s = jnp.einsum('grd,gdk->grk', qf.astype(BF16), kc_ref[...].astype(BF16),
                   preferred_element_type=F32)
    s_new = jnp.sum(qf * knew_ref[...], axis=-1, keepdims=True)
    gidx = jax.lax.broadcasted_iota(jnp.int32, (S2_GROUPS, Q_PER_KV, 1), 0)
    ridx = jax.lax.broadcasted_iota(jnp.int32, (S2_GROUPS, Q_PER_KV, 1), 1)
    head = (gidx % N_KV_HEADS) * Q_PER_KV + ridx
    sink = jnp.zeros((S2_GROUPS, Q_PER_KV, 1), F32)
    for hh in range(N_HEADS):
        sink = jnp.where(head == hh, sinks_ref[layer, hh], sink)
    mx = jnp.maximum(jnp.maximum(jnp.max(s, axis=-1, keepdims=True), s_new), sink)
    e = jnp.exp(s - mx)
    e_new = jnp.exp(s_new - mx)
    den = jnp.sum(e, axis=-1, keepdims=True) + e_new + jnp.exp(sink - mx)
    o = jnp.einsum('grk,gdk->grd', e.astype(BF16), vc_ref[...].astype(BF16),
                   preferred_element_type=F32)
    o_ref[...] = (o + e_new * vnew_ref[...]) * (1.0 / den)

    steps_per_tile = LANES // S2_GROUPS
    tile = pl.multiple_of((step // steps_per_tile) * LANES, LANES)
    first_col = (step % steps_per_tile) * S2_GROUPS
    last_lane = _lane_iota((HEAD_DIM, WINDOW)) == WINDOW - 1
    for new_t_ref, cache_ref, out_ref in ((knew_t_ref, kc_ref, kout_ref),
                                          (vnew_t_ref, vc_ref, vout_ref)):
        new_cols = new_t_ref[:, pl.ds(tile, LANES)]
        for j in range(S2_GROUPS):
            col = pltpu.roll(new_cols, WINDOW - 1 - first_col - j, axis=1)
            out_ref[j] = jnp.where(last_lane, col, pltpu.roll(cache_ref[j], WINDOW - 1, axis=1))


def _sample_attn(layer, sinks, q, knew, vnew, knew_t, vnew_t, kcache, vcache, kout_prev, vout_prev):
    groups = q.shape[0]
    group3 = lambda d1, d2: pl.BlockSpec((S2_GROUPS, d1, d2), lambda i, *_: (i, 0, 0))
    cache_blk = pl.BlockSpec((None, S2_GROUPS, HEAD_DIM, WINDOW), lambda i, *_: (layer, i, 0, 0))
    aliased = [] if kout_prev is None else [kout_prev, vout_prev]
    n_al = len(aliased)
    in_specs = ([pl.BlockSpec(memory_space=pl.ANY)] * n_al +
                [group3(Q_PER_KV, HEAD_DIM), group3(1, HEAD_DIM), group3(1, HEAD_DIM),
                 _whole(knew_t.shape), _whole(vnew_t.shape), cache_blk, cache_blk])
    io_alias = {1: 1, 2: 2} if n_al else {}
    return pl.pallas_call(
        functools.partial(_sample_attn_kernel, layer, n_al),
        grid_spec=pltpu.PrefetchScalarGridSpec(
            num_scalar_prefetch=1, grid=(groups // S2_GROUPS,),
            in_specs=in_specs,
            out_specs=[group3(Q_PER_KV, HEAD_DIM), cache_blk, cache_blk]),
        out_shape=[jax.ShapeDtypeStruct((groups, Q_PER_KV, HEAD_DIM), F32),
                   jax.ShapeDtypeStruct(kcache.shape, F32),
                   jax.ShapeDtypeStruct(vcache.shape, F32)],
        input_output_aliases=io_alias,
        compiler_params=pltpu.CompilerParams(
            dimension_semantics=("arbitrary",), vmem_limit_bytes=V7X_VMEM_LIMIT),
        name="sample_attn",
    )(sinks, *aliased, q, knew, vnew, knew_t, vnew_t, kcache, vcache)


def _sample_out_kernel(x_ref, o_ref, mab_ref, gc_ref, gpost_ref, gpre2_ref, gpost2_ref,
                       wc_ref, wo_ref, wg_ref, wu_ref, wd_ref, y_ref):
    y_c = _dot(o_ref[...].astype(BF16), wc_ref[...])
    merged = mab_ref[...] + gc_ref[...] * y_c
    x1 = x_ref[...] + _rmsnorm(_dot(merged.astype(BF16), wo_ref[...]), gpost_ref[...])
    y_ref[...] = _swiglu(x1, gpre2_ref[...], gpost2_ref[...], wg_ref, wu_ref, wd_ref)


def _sample_out(layer, x, o, mab, gc, gpost, gpre2, gpost2, wc, wo, wg, wu, wd):
    lb = functools.partial(_layer_block, layer)
    in_specs = [_whole(x.shape), _whole(o.shape), _whole(mab.shape), _whole(gc.shape),
                lb((1, D_MODEL)), lb((1, D_MODEL)), lb((1, D_MODEL)),
                _whole((D_MODEL, D_MODEL)), _whole((D_MODEL, D_MODEL)),
                _whole((D_MODEL, D_FF)), _whole((D_MODEL, D_FF)), _whole((D_FF, D_MODEL))]
    return pl.pallas_call(
        _sample_out_kernel,
        grid=(1,),
        in_specs=in_specs,
        out_specs=pl.BlockSpec(x.shape, lambda i: (0, 0)),
        out_shape=jax.ShapeDtypeStruct(x.shape, F32),
        compiler_params=pltpu.CompilerParams(
            dimension_semantics=("arbitrary",), vmem_limit_bytes=V7X_VMEM_LIMIT),
        name="sample_out",
    )(x, o, mab, gc, gpost, gpre2, gpost2, wc, wo, wg, wu, wd)


def _rope_angles(pos):
    half = HEAD_DIM // 2
    inv = jnp.power(jnp.float32(ROPE_THETA), -jnp.arange(half, dtype=F32) * (2.0 / HEAD_DIM))
    return pos.astype(F32)[:, None] * jnp.tile(inv, LANES // half)[None, :]


def _rope_sign():
    lane = jnp.arange(LANES, dtype=jnp.int32)
    return jnp.where((lane % HEAD_DIM) < (HEAD_DIM // 2), -1.0, 1.0).astype(F32)[None, :]


def kernel(x_prompt, x_sample, state_conv, cache_win_k, cache_win_v, norm_pre_mix, norm_post_mix,
           norm_pre_ffn, norm_post_ffn, w_in, chunk_ln_g, chunk_ln_b, w_spatial, b_spatial, conv_w,
           conv_b, attn_sinks, w_br_a, w_br_b, w_br_c, w_out, w_ffn_gate, w_ffn_up, w_ffn_down):
    xp = x_prompt.reshape(SEQ, D_MODEL)
    xs = x_sample.reshape(DEC_BATCH, D_MODEL)

    sign = _rope_sign()
    ang_a = _rope_angles(jnp.arange(SEQ // TM_MIX, dtype=jnp.int32) * TM_MIX)
    ang_b = _rope_angles(jnp.arange(TM_MIX, dtype=jnp.int32))
    rope_p = (jnp.cos(ang_a)[:, None, :], jnp.sin(ang_a)[:, None, :],
              jnp.cos(ang_b), jnp.sin(ang_b), sign * jnp.cos(ang_b), sign * jnp.sin(ang_b))
    ang_s = _rope_angles(jnp.full((1,), PAST_LEN, dtype=jnp.int32))
    cos_s, sin_s = jnp.cos(ang_s), sign * jnp.sin(ang_s)

    mixer_f32 = (w_in, w_br_a, w_br_b, w_br_c, w_out)
    ffn_f32 = (w_ffn_gate, w_ffn_up, w_ffn_down)
    mixer_w = tuple(w[0].astype(BF16) for w in mixer_f32)
    rows = lambda a: a.reshape(DEPTH, 1, -1)
    gpre, gpost = rows(norm_pre_mix), rows(norm_post_mix)
    gpre2, gpost2 = rows(norm_pre_ffn), rows(norm_post_ffn)
    lng, lnb, convb = rows(chunk_ln_g), rows(chunk_ln_b), rows(conv_b)
    bsp = jnp.repeat(jnp.transpose(b_spatial, (0, 2, 1)), A_GROUP_DIM, axis=2)
    wsp0 = rows(jnp.repeat(w_spatial[:, :, 0, 0], A_GROUP_DIM, axis=1))
    bsp0 = rows(jnp.repeat(b_spatial[:, :, 0], A_GROUP_DIM, axis=1))
    hist = state_conv.reshape(DEPTH, DEC_BATCH, (CONV_W - 1) * D_MODEL)
    n_groups = DEC_BATCH * N_KV_HEADS
    to_groups = lambda c: jnp.transpose(c, (0, 1, 3, 4, 2)).reshape(DEPTH, n_groups, HEAD_DIM, WINDOW)
    kcache, vcache = to_groups(cache_win_k), to_groups(cache_win_v)

    p_conv, p_k, p_v = [], [], []
    s_conv, s_cv = [], []
    knext = vnext = None
    for l in range(DEPTH):
        w_in_b, wa, wb, wc, wo = mixer_w
        x1, cxlast, klast, vlast, wg, wu, wd = _prompt_mixer(
            l, xp, rope_p, attn_sinks, gpre, gpost, w_in_b, lng, lnb, w_spatial, bsp,
            conv_w, convb, wa, wb, wc, wo, l, ffn_f32)
        if l + 1 < DEPTH:
            xp, *mixer_w = _prompt_ffn(l, x1, gpre2, gpost2, wg, wu, wd, l + 1, mixer_f32)
        else:
            (xp,) = _prompt_ffn(l, x1, gpre2, gpost2, wg, wu, wd, 0, ())
        p_conv.append(cxlast[SUBLANES - (CONV_W - 1):].reshape(1, CONV_W - 1, D_MODEL))
        p_k.append(klast.reshape(1, WINDOW, N_KV_HEADS, HEAD_DIM))
        p_v.append(vlast.reshape(1, WINDOW, N_KV_HEADS, HEAD_DIM))

        va, newconv, q, knew, vnew, mab, gc = _sample_proj(
            l, xs, hist, cos_s, sin_s, gpre, w_in_b, lng, lnb, wsp0, bsp0, conv_w, convb, wa, wb)
        knew = knew.reshape(n_groups, HEAD_DIM)
        vnew = vnew.reshape(n_groups, HEAD_DIM)
        o, knext, vnext = _sample_attn(
            l, attn_sinks, q.reshape(n_groups, Q_PER_KV, HEAD_DIM),
            knew[:, None, :], vnew[:, None, :], jnp.transpose(knew), jnp.transpose(vnew),
            kcache, vcache, knext, vnext)
        xs = _sample_out(l, xs, o.reshape(DEC_BATCH, D_MODEL), mab, gc, gpost, gpre2, gpost2,
                         wc, wo, wg, wu, wd)
        s_conv.append(newconv.reshape(DEC_BATCH, CONV_W - 1, D_MODEL))
        s_cv.append(va.reshape(DEC_BATCH, 1, D_MODEL))

    from_groups = lambda c: jnp.transpose(
        c.reshape(DEPTH, DEC_BATCH, N_KV_HEADS, HEAD_DIM, WINDOW), (0, 1, 4, 2, 3))
    return (xp.reshape(1, SEQ, D_MODEL), xs.reshape(DEC_BATCH, 1, D_MODEL),
            jnp.stack(p_conv), jnp.stack(p_k), jnp.stack(p_v),
            jnp.stack(s_conv), from_groups(knext), from_groups(vnext), jnp.stack(s_cv))
```

```python
import functools

import jax
import jax.numpy as jnp
from jax.experimental import pallas as pl
from jax.experimental.pallas import tpu as pltpu

D_MODEL = 1024
SEQ = 16384
DEPTH = 2
DEC_BATCH = 128
PAST_LEN = 16384
CHUNK = 128
A_GROUPS = 8
A_GROUP_DIM = D_MODEL // A_GROUPS
CONV_W = 3
N_HEADS = 16
N_KV_HEADS = 4
HEAD_DIM = 64
Q_PER_KV = N_HEADS // N_KV_HEADS
WINDOW = 128
ROPE_THETA = 10000.0
D_FF = 2816
KV_DIM = N_KV_HEADS * HEAD_DIM
NEG_INF = -1e30
LOG2_E = 1.4426950408889634

OFF_UA = 0
OFF_VA = OFF_UA + D_MODEL
OFF_BG = OFF_VA + D_MODEL
OFF_CG = OFF_BG + D_MODEL
OFF_HB = OFF_CG + D_MODEL
OFF_Q = OFF_HB + D_MODEL
OFF_K = OFF_Q + N_HEADS * HEAD_DIM
OFF_V = OFF_K + KV_DIM
OFF_GA = OFF_V + KV_DIM
OFF_GB = OFF_GA + D_MODEL
OFF_GC = OFF_GB + D_MODEL
IN_WIDTH = OFF_GC + D_MODEL

LANES = 128
SUBLANES = 8
BF16_SUBLANES = 16
V7X_VMEM_LIMIT = 62 * 1024 * 1024

TM_MIX = 512
TM_FFN = 1024
FFN_ROWS = 256
BLK = WINDOW
S2_GROUPS = 64
ATTN_LOOKAHEAD = 2
GATE_COLS = 256
PIECES_BEFORE_ATTN = 3

assert CHUNK == BLK and WINDOW == BLK and PAST_LEN >= WINDOW
assert 2 * HEAD_DIM == LANES and A_GROUP_DIM == LANES
assert SEQ % TM_MIX == 0 and SEQ % TM_FFN == 0 and TM_MIX % BLK == 0
assert TM_FFN % FFN_ROWS == 0
assert LANES % S2_GROUPS == 0 and S2_GROUPS % N_KV_HEADS == 0 and WINDOW == LANES
assert (DEC_BATCH * N_KV_HEADS) % LANES == 0

BF16 = jnp.bfloat16
F32 = jnp.float32


def _dot(a, b):
    return jnp.dot(a, b, preferred_element_type=F32)


def _dot_nt(a, b):
    return jax.lax.dot_general(a, b, (((1,), (1,)), ((), ())), preferred_element_type=F32)


def _rmsnorm(x, g, eps=1e-6):
    return x * jax.lax.rsqrt(jnp.mean(x * x, axis=-1, keepdims=True) + eps) * g


def _layernorm(x, g, b, eps=1e-5):
    mu = jnp.mean(x, axis=-1, keepdims=True)
    xc = x - mu
    var = jnp.mean(xc * xc, axis=-1, keepdims=True)
    return xc * jax.lax.rsqrt(var + eps) * g + b


def _lane_iota(shape):
    return jax.lax.broadcasted_iota(jnp.int32, shape, len(shape) - 1)


def _rope_block(xb, cos_t, sin_s):
    lane = _lane_iota(xb.shape)
    first_half = (lane % HEAD_DIM) < (HEAD_DIM // 2)
    rot = jnp.where(first_half,
                    pltpu.roll(xb, LANES - HEAD_DIM // 2, axis=1),
                    pltpu.roll(xb, HEAD_DIM // 2, axis=1))
    return xb * cos_t + rot * sin_s


def _rope(x, cos_t, sin_s):
    n = x.shape[1] // LANES
    return jnp.concatenate(
        [_rope_block(x[:, j * LANES:(j + 1) * LANES], cos_t, sin_s) for j in range(n)], axis=1)


def _split_kv_heads(blk):
    out = []
    for j in range(KV_DIM // LANES):
        b = blk[:, j * LANES:(j + 1) * LANES]
        r = pltpu.roll(b, HEAD_DIM, axis=1)
        low = _lane_iota(b.shape) < HEAD_DIM
        zero = jnp.zeros_like(b)
        out.append((jnp.where(low, b, zero).astype(BF16), jnp.where(low, zero, r).astype(BF16)))
        out.append((jnp.where(low, r, zero).astype(BF16), jnp.where(low, zero, b).astype(BF16)))
    return out


def _layer_block(layer, shape, single_buffer=True):
    nd = len(shape)
    mode = pl.Buffered(1) if single_buffer else None
    return pl.BlockSpec((None,) + tuple(shape), lambda *_: (layer,) + (0,) * nd, pipeline_mode=mode)


def _whole(shape):
    nd = len(shape)
    return pl.BlockSpec(tuple(shape), lambda *_: (0,) * nd, pipeline_mode=pl.Buffered(1))


N_MIXER_INPUTS = 20


def _cast_plan(rows, n_steps):
    for nb in range(n_steps, 0, -1):
        if rows % nb == 0 and (rows // nb) % BF16_SUBLANES == 0:
            return nb
    raise ValueError(f"no bf16-aligned row split of {rows} rows over {n_steps} steps")


def _cast_specs(layer, w_stacked, n_steps):
    _, rows, cols = w_stacked.shape
    nb = _cast_plan(rows, n_steps)
    blk = lambda i: jnp.minimum(i, nb - 1)
    return (pl.BlockSpec((None, rows // nb, cols), lambda i, *_: (layer, blk(i), 0)),
            pl.BlockSpec((rows // nb, cols), lambda i, *_: (blk(i), 0)),
            jax.ShapeDtypeStruct((rows, cols), BF16))


def _cast_row_blocks(src_refs, dst_refs):
    for src, dst in zip(src_refs, dst_refs):
        dst[...] = src[...].astype(BF16)


def _mixer_kernel(layer, n_cast, sinks_ref, *refs):
    (x_ref, cos_a_ref, sin_a_ref, cos_b_ref, sin_b_ref, cos_bs_ref, sin_bs_ref,
     gpre_ref, gpost_ref, w_in_ref, lng_ref, lnb_ref, wsp_ref, bsp_ref, convw_ref, convb_ref,
     wa_ref, wb_ref, wc_ref, wo_ref) = refs[:N_MIXER_INPUTS]
    cast_src = refs[N_MIXER_INPUTS:N_MIXER_INPUTS + n_cast]
    outs = refs[N_MIXER_INPUTS + n_cast:]
    x1_ref, cxlast_ref, klast_ref, vlast_ref = outs[:4]
    cast_dst = outs[4:4 + n_cast]
    wsp_s, kprev_s, vprev_s, cxprev_s = outs[4 + n_cast:]
    step = pl.program_id(0)
    nblk = TM_MIX // BLK

    @pl.when(step == 0)
    def _init():
        row = jax.lax.broadcasted_iota(jnp.int32, (CHUNK, CHUNK), 0)
        col = jax.lax.broadcasted_iota(jnp.int32, (CHUNK, CHUNK), 1)
        for g in range(A_GROUPS):
            wsp_s[g] = jnp.where(col <= row, wsp_ref[g], 0.0).astype(BF16)
        kprev_s[...] = jnp.zeros_like(kprev_s)
        vprev_s[...] = jnp.zeros_like(vprev_s)
        cxprev_s[...] = jnp.zeros_like(cxprev_s)

    x = x_ref[...]
    h = _rmsnorm(x, gpre_ref[...]).astype(BF16)

    def proj(off, width):
        return _dot(h, w_in_ref[:, off:off + width])

    va = _layernorm(jax.nn.gelu(proj(OFF_VA, D_MODEL)), lng_ref[...], lnb_ref[...]).astype(BF16)
    ua = jax.nn.gelu(proj(OFF_UA, D_MODEL))
    cx = proj(OFF_CG, D_MODEL) * proj(OFF_HB, D_MODEL)
    _cast_row_blocks(cast_src, cast_dst)
    prev = cxprev_s[...]
    rowi = jax.lax.broadcasted_iota(jnp.int32, cx.shape, 0)
    cx1 = jnp.where(rowi < 1, prev[SUBLANES - 1:SUBLANES, :], pltpu.roll(cx, 1, axis=0))
    cx2 = jnp.where(rowi < 2,
                    jnp.where(rowi < 1, prev[SUBLANES - 2:SUBLANES - 1, :],
                              prev[SUBLANES - 1:SUBLANES, :]),
                    pltpu.roll(cx, 2, axis=0))
    conv = convb_ref[...] + convw_ref[0:1, :] * cx2
    conv = conv + convw_ref[1:2, :] * cx1
    conv = conv + convw_ref[2:3, :] * cx
    cxprev_s[...] = cx[TM_MIX - SUBLANES:, :]
    cxlast_ref[...] = cx[TM_MIX - SUBLANES:, :]
    bx = (proj(OFF_BG, D_MODEL) * conv).astype(BF16)
    sa_rows = []
    for c in range(nblk):
        sa_rows.append(jnp.concatenate(
            [_dot(wsp_s[g], va[c * BLK:(c + 1) * BLK, g * LANES:(g + 1) * LANES])
             for g in range(A_GROUPS)], axis=1) + bsp_ref[...])
    ax = (ua * jnp.concatenate(sa_rows, axis=0)).astype(BF16)

    cos_a, sin_a = cos_a_ref[...], sin_a_ref[...]
    cos_t = cos_a * cos_b_ref[...] - sin_a * sin_b_ref[...]
    sin_s = sin_a * cos_bs_ref[...] + cos_a * sin_bs_ref[...]
    q = (_rope(proj(OFF_Q, N_HEADS * HEAD_DIM), cos_t, sin_s)
         * (LOG2_E * HEAD_DIM ** -0.5)).astype(BF16)
    k = _rope(proj(OFF_K, KV_DIM), cos_t, sin_s)
    v = proj(OFF_V, KV_DIM)
    klast_ref[...] = k[TM_MIX - BLK:, :]
    vlast_ref[...] = v[TM_MIX - BLK:, :]

    merged_ab, gate_c = [], []

    def gated_branch_pieces():
        for j in range(D_MODEL // GATE_COLS):
            cols = slice(j * GATE_COLS, (j + 1) * GATE_COLS)
            y_a = _dot(ax, wa_ref[:, cols])
            yield
            part = jax.nn.sigmoid(proj(OFF_GA + j * GATE_COLS, GATE_COLS)) * y_a
            yield
            y_b = _dot(bx, wb_ref[:, cols])
            yield
            merged_ab.append(part + jax.nn.sigmoid(proj(OFF_GB + j * GATE_COLS, GATE_COLS)) * y_b)
            yield
            gate_c.append(jax.nn.sigmoid(proj(OFF_GC + j * GATE_COLS, GATE_COLS)))
            yield

    pieces = gated_branch_pieces()
    n_pieces = 5 * (D_MODEL // GATE_COLS)
    for _ in range(PIECES_BEFORE_ATTN):
        next(pieces)

    qi = jax.lax.broadcasted_iota(jnp.int32, (BLK, 2 * BLK), 0)
    kj = jax.lax.broadcasted_iota(jnp.int32, (BLK, 2 * BLK), 1)
    band = (kj >= qi) & (kj <= qi + WINDOW)
    lane_low = _lane_iota((BLK, LANES)) < HEAD_DIM

    k_parts = [[(kprev_s[g, 0], kprev_s[g, 1]) for g in range(N_KV_HEADS)]]
    v_parts = [[(vprev_s[g, 0], vprev_s[g, 1]) for g in range(N_KV_HEADS)]]
    for c in range(nblk):
        k_parts.append(_split_kv_heads(k[c * BLK:(c + 1) * BLK, :]))
        v_parts.append(_split_kv_heads(v[c * BLK:(c + 1) * BLK, :]))
    for g in range(N_KV_HEADS):
        kprev_s[g, 0] = k_parts[nblk][g][0]
        kprev_s[g, 1] = k_parts[nblk][g][1]
        vprev_s[g, 0] = v_parts[nblk][g][0]
        vprev_s[g, 1] = v_parts[nblk][g][1]
    mask_first = band & (kj >= jnp.where(step == 0, BLK, 0))

    def block_diag(parts, c, g):
        return jnp.concatenate([parts[c][g][0], parts[c + 1][g][0],
                                parts[c][g][1], parts[c + 1][g][1]], axis=0)

    def scores(c, m):
        kbd = block_diag(k_parts, c, (2 * m) // Q_PER_KV)
        return _dot_nt(q[c * BLK:(c + 1) * BLK, m * LANES:(m + 1) * LANES], kbd)

    def softmax_numerators(c, m, s):
        mask = mask_first if c == 0 else band
        es, invs = [], []
        for hh in range(2):
            sink = sinks_ref[layer, 2 * m + hh] * LOG2_E
            sh = jnp.where(mask, s[:, hh * 2 * BLK:(hh + 1) * 2 * BLK], NEG_INF)
            mx = jnp.maximum(jnp.max(sh, axis=-1, keepdims=True), sink)
            e = jnp.exp2(sh - mx)
            den = jnp.sum(e, axis=-1, keepdims=True) + jnp.exp2(sink - mx)
            es.append(e.astype(BF16))
            invs.append(1.0 / den)
        return jnp.concatenate(es, axis=1), jnp.where(lane_low, invs[0], invs[1])

    def weighted_values(c, m, e, inv):
        vbd = block_diag(v_parts, c, (2 * m) // Q_PER_KV)
        return _dot(e, vbd) * inv

    its = [(c, m) for c in range(nblk) for m in range(N_HEADS // 2)]
    s_queue = [scores(*its[i]) for i in range(min(ATTN_LOOKAHEAD, len(its)))]
    o_parts = {}
    issued = PIECES_BEFORE_ATTN
    for i, (c, m) in enumerate(its):
        if i + ATTN_LOOKAHEAD < len(its):
            s_queue.append(scores(*its[i + ATTN_LOOKAHEAD]))
        e, inv = softmax_numerators(c, m, s_queue.pop(0))
        o_parts[(c, m)] = weighted_values(c, m, e, inv)
        while issued < PIECES_BEFORE_ATTN + ((i + 1) * (n_pieces - PIECES_BEFORE_ATTN)) // len(its):
            next(pieces)
            issued += 1
    o = jnp.concatenate(
        [jnp.concatenate([o_parts[(c, m)] for m in range(N_HEADS // 2)], axis=1)
         for c in range(nblk)], axis=0).astype(BF16)
    y_c = _dot(o, wc_ref[...])
    merged = jnp.concatenate(merged_ab, axis=1) + jnp.concatenate(gate_c, axis=1) * y_c

    merged = merged.astype(BF16)
    for c in range(nblk):
        rows = slice(c * BLK, (c + 1) * BLK)
        x1_ref[rows, :] = x[rows, :] + _rmsnorm(_dot(merged[rows, :], wo_ref[...]), gpost_ref[...])


def _prompt_mixer(layer, x, rope, sinks, gpre, gpost, w_in, lng, lnb, wsp, bsp, convw, convb,
                  wa, wb, wc, wo, cast_layer, cast_weights):
    n = x.shape[0]
    n_steps = n // TM_MIX
    lb = functools.partial(_layer_block, layer)
    casts = [_cast_specs(cast_layer, w, n_steps) for w in cast_weights]
    in_specs = [
        pl.BlockSpec((TM_MIX, D_MODEL), lambda i, *_: (i, 0)),
        pl.BlockSpec((None, 1, LANES), lambda i, *_: (i, 0, 0)),
        pl.BlockSpec((None, 1, LANES), lambda i, *_: (i, 0, 0)),
        _whole((TM_MIX, LANES)), _whole((TM_MIX, LANES)),
        _whole((TM_MIX, LANES)), _whole((TM_MIX, LANES)),
        lb((1, D_MODEL)), lb((1, D_MODEL)),
        _whole((D_MODEL, IN_WIDTH)),
        lb((1, D_MODEL)), lb((1, D_MODEL)),
        lb((A_GROUPS, CHUNK, CHUNK)), lb((CHUNK, D_MODEL)),
        lb((CONV_W, D_MODEL)), lb((1, D_MODEL)),
        _whole((D_MODEL, D_MODEL)), _whole((D_MODEL, D_MODEL)),
        _whole((D_MODEL, D_MODEL)), _whole((D_MODEL, D_MODEL)),
    ] + [c[0] for c in casts]
    assert len(in_specs) == N_MIXER_INPUTS + len(casts)
    out_specs = [
        pl.BlockSpec((TM_MIX, D_MODEL), lambda i, *_: (i, 0)),
        pl.BlockSpec((SUBLANES, D_MODEL), lambda i, *_: (0, 0)),
        pl.BlockSpec((BLK, KV_DIM), lambda i, *_: (0, 0)),
        pl.BlockSpec((BLK, KV_DIM), lambda i, *_: (0, 0)),
    ] + [c[1] for c in casts]
    out_shape = [
        jax.ShapeDtypeStruct((n, D_MODEL), F32),
        jax.ShapeDtypeStruct((SUBLANES, D_MODEL), F32),
        jax.ShapeDtypeStruct((BLK, KV_DIM), F32),
        jax.ShapeDtypeStruct((BLK, KV_DIM), F32),
    ] + [c[2] for c in casts]
    scratch = [
        pltpu.VMEM((A_GROUPS, CHUNK, CHUNK), BF16),
        pltpu.VMEM((N_KV_HEADS, 2, BLK, LANES), BF16),
        pltpu.VMEM((N_KV_HEADS, 2, BLK, LANES), BF16),
        pltpu.VMEM((SUBLANES, D_MODEL), F32),
    ]
    return pl.pallas_call(
        functools.partial(_mixer_kernel, layer, len(casts)),
        grid_spec=pltpu.PrefetchScalarGridSpec(
            num_scalar_prefetch=1, grid=(n_steps,), in_specs=in_specs, out_specs=out_specs,
            scratch_shapes=scratch),
        out_shape=out_shape,
        compiler_params=pltpu.CompilerParams(
            dimension_semantics=("arbitrary",), vmem_limit_bytes=V7X_VMEM_LIMIT),
        name="prompt_mixer",
    )(sinks, x, *rope, gpre, gpost, w_in, lng, lnb, wsp, bsp, convw, convb, wa, wb, wc, wo,
      *cast_weights)


def _swiglu(x, gpre, gpost, wg_ref, wu_ref, wd_ref):
    h = _rmsnorm(x, gpre).astype(BF16)
    a = jax.nn.silu(_dot(h, wg_ref[...])) * _dot(h, wu_ref[...])
    return x + _rmsnorm(_dot(a.astype(BF16), wd_ref[...]), gpost)


N_FFN_INPUTS = 6


def _ffn_kernel(n_cast, *refs):
    x_ref, gpre_ref, gpost_ref, wg_ref, wu_ref, wd_ref = refs[:N_FFN_INPUTS]
    cast_src = refs[N_FFN_INPUTS:N_FFN_INPUTS + n_cast]
    y_ref = refs[N_FFN_INPUTS + n_cast]
    cast_dst = refs[N_FFN_INPUTS + n_cast + 1:]
    rows = [slice(r * FFN_ROWS, (r + 1) * FFN_ROWS) for r in range(TM_FFN // FFN_ROWS)]
    xs = [x_ref[r, :] for r in rows]
    acts = []
    for x in xs:
        h = _rmsnorm(x, gpre_ref[...]).astype(BF16)
        acts.append((jax.nn.silu(_dot(h, wg_ref[...])) * _dot(h, wu_ref[...])).astype(BF16))
    for r, x, a in zip(rows, xs, acts):
        y_ref[r, :] = x + _rmsnorm(_dot(a, wd_ref[...]), gpost_ref[...])
    _cast_row_blocks(cast_src, cast_dst)


def _prompt_ffn(layer, x, gpre, gpost, wg, wu, wd, cast_layer, cast_weights):
    n = x.shape[0]
    n_steps = n // TM_FFN
    lb = functools.partial(_layer_block, layer)
    casts = [_cast_specs(cast_layer, w, n_steps) for w in cast_weights]
    row_tile = pl.BlockSpec((TM_FFN, D_MODEL), lambda i: (i, 0))
    return pl.pallas_call(
        functools.partial(_ffn_kernel, len(casts)),
        grid=(n_steps,),
        in_specs=[row_tile, lb((1, D_MODEL)), lb((1, D_MODEL)),
                  _whole((D_MODEL, D_FF)), _whole((D_MODEL, D_FF)), _whole((D_FF, D_MODEL))]
                 + [c[0] for c in casts],
        out_specs=[row_tile] + [c[1] for c in casts],
        out_shape=[jax.ShapeDtypeStruct((n, D_MODEL), F32)] + [c[2] for c in casts],
        compiler_params=pltpu.CompilerParams(
            dimension_semantics=("arbitrary",), vmem_limit_bytes=V7X_VMEM_LIMIT),
        name="prompt_ffn",
    )(x, gpre, gpost, wg, wu, wd, *cast_weights)


def _sample_proj_kernel(x_ref, hist_ref, cos_ref, sin_ref, gpre_ref, w_in_ref, lng_ref, lnb_ref,
                        wsp0_ref, bsp0_ref, convw_ref, convb_ref, wa_ref, wb_ref,
                        va_ref, newconv_ref, q_ref, k_ref, v_ref, mab_ref, gc_ref):
    x = x_ref[...]
    h = _rmsnorm(x, gpre_ref[...]).astype(BF16)

    def proj(off, width):
        return _dot(h, w_in_ref[:, off:off + width])

    ua = jax.nn.gelu(proj(OFF_UA, D_MODEL))
    va = _layernorm(jax.nn.gelu(proj(OFF_VA, D_MODEL)), lng_ref[...], lnb_ref[...])
    va_ref[...] = va
    sa = wsp0_ref[...] * va + bsp0_ref[...]
    y_a = _dot((ua * sa).astype(BF16), wa_ref[...])
    merged = jax.nn.sigmoid(proj(OFF_GA, D_MODEL)) * y_a

    cx = proj(OFF_CG, D_MODEL) * proj(OFF_HB, D_MODEL)
    h0 = hist_ref[:, 0:D_MODEL]
    h1 = hist_ref[:, D_MODEL:2 * D_MODEL]
    conv = convb_ref[...] + convw_ref[0:1, :] * h0
    conv = conv + convw_ref[1:2, :] * h1
    conv = conv + convw_ref[2:3, :] * cx
    newconv_ref[:, 0:D_MODEL] = h1
    newconv_ref[:, D_MODEL:2 * D_MODEL] = cx
    y_b = _dot((proj(OFF_BG, D_MODEL) * conv).astype(BF16), wb_ref[...])
    mab_ref[...] = merged + jax.nn.sigmoid(proj(OFF_GB, D_MODEL)) * y_b
    gc_ref[...] = jax.nn.sigmoid(proj(OFF_GC, D_MODEL))

    cos_t = cos_ref[...]
    sin_s = sin_ref[...]
    q_ref[...] = _rope(proj(OFF_Q, N_HEADS * HEAD_DIM), cos_t, sin_s) * (HEAD_DIM ** -0.5)
    k_ref[...] = _rope(proj(OFF_K, KV_DIM), cos_t, sin_s)
    v_ref[...] = proj(OFF_V, KV_DIM)


def _sample_proj(layer, x, hist, cos_t, sin_s, gpre, w_in, lng, lnb, wsp0, bsp0, convw, convb,
                 wa, wb):
    b = x.shape[0]
    lb = functools.partial(_layer_block, layer)
    in_specs = [
        _whole(x.shape), lb((b, (CONV_W - 1) * D_MODEL)), _whole(cos_t.shape), _whole(sin_s.shape),
        lb((1, D_MODEL)), _whole((D_MODEL, IN_WIDTH)), lb((1, D_MODEL)), lb((1, D_MODEL)),
        lb((1, D_MODEL)), lb((1, D_MODEL)), lb((CONV_W, D_MODEL)), lb((1, D_MODEL)),
        _whole((D_MODEL, D_MODEL)), _whole((D_MODEL, D_MODEL)),
    ]
    out_shape = [
        jax.ShapeDtypeStruct((b, D_MODEL), F32),
        jax.ShapeDtypeStruct((b, 2 * D_MODEL), F32),
        jax.ShapeDtypeStruct((b, D_MODEL), F32),
        jax.ShapeDtypeStruct((b, KV_DIM), F32),
        jax.ShapeDtypeStruct((b, KV_DIM), F32),
        jax.ShapeDtypeStruct((b, D_MODEL), F32),
        jax.ShapeDtypeStruct((b, D_MODEL), F32),
    ]
    return pl.pallas_call(
        _sample_proj_kernel,
        grid=(1,),
        in_specs=in_specs,
        out_specs=[pl.BlockSpec(s.shape, lambda i: (0, 0)) for s in out_shape],
        out_shape=out_shape,
        compiler_params=pltpu.CompilerParams(
            dimension_semantics=("arbitrary",), vmem_limit_bytes=V7X_VMEM_LIMIT),
        name="sample_proj",
    )(x, hist, cos_t, sin_s, gpre, w_in, lng, lnb, wsp0, bsp0, convw, convb, wa, wb)


def _sample_attn_kernel(layer, n_aliased, sinks_ref, *refs):
    (q_ref, knew_ref, vnew_ref, knew_t_ref, vnew_t_ref, kc_ref, vc_ref,
     o_ref, kout_ref, vout_ref) = refs[n_aliased:]
    step = pl.program_id(0)
    qf = q_ref[...]
    s = jnp.einsum('grd,gdk->grk', qf.astype(BF16), kc_ref[...].astype(BF16),
                   preferred_element_type=F32)
    s_new = jnp.sum(qf * knew_ref[...], axis=-1, keepdims=True)
    gidx = jax.lax.broadcasted_iota(jnp.int32, (S2_GROUPS, Q_PER_KV, 1), 0)
    ridx = jax.lax.broadcasted_iota(jnp.int32, (S2_GROUPS, Q_PER_KV, 1), 1)
    head = (gidx % N_KV_HEADS) * Q_PER_KV + ridx
    sink = jnp.zeros((S2_GROUPS, Q_PER_KV, 1), F32)
    for hh in range(N_HEADS):
        sink = jnp.where(head == hh, sinks_ref[layer, hh], sink)
    mx = jnp.maximum(jnp.maximum(jnp.max(s, axis=-1, keepdims=True), s_new), sink)
    e = jnp.exp(s - mx)
    e_new = jnp.exp(s_new - mx)
    den = jnp.sum(e, axis=-1, keepdims=True) + e_new + jnp.exp(sink - mx)
    o = jnp.einsum('grk,gdk->grd', e.astype(BF16), vc_ref[...].astype(BF16),
                   preferred_element_type=F32)
    o_ref[...] = (o + e_new * vnew_ref[...]) * (1.0 / den)

    steps_per_tile = LANES // S2_GROUPS
    tile = pl.multiple_of((step // steps_per_tile) * LANES, LANES)
    first_col = (step % steps_per_tile) * S2_GROUPS
    last_lane = _lane_iota((HEAD_DIM, WINDOW)) == WINDOW - 1
    for new_t_ref, cache_ref, out_ref in ((knew_t_ref, kc_ref, kout_ref),
                                          (vnew_t_ref, vc_ref, vout_ref)):
        new_cols = new_t_ref[:, pl.ds(tile, LANES)]
        for j in range(S2_GROUPS):
            col = pltpu.roll(new_cols, WINDOW - 1 - first_col - j, axis=1)
            out_ref[j] = jnp.where(last_lane, col, pltpu.roll(cache_ref[j], WINDOW - 1, axis=1))


def _sample_attn(layer, sinks, q, knew, vnew, knew_t, vnew_t, kcache, vcache, kout_prev, vout_prev):
    groups = q.shape[0]
    group3 = lambda d1, d2: pl.BlockSpec((S2_GROUPS, d1, d2), lambda i, *_: (i, 0, 0))
    cache_blk = pl.BlockSpec((None, S2_GROUPS, HEAD_DIM, WINDOW), lambda i, *_: (layer, i, 0, 0))
    aliased = [] if kout_prev is None else [kout_prev, vout_prev]
    n_al = len(aliased)
    in_specs = ([pl.BlockSpec(memory_space=pl.ANY)] * n_al +
                [group3(Q_PER_KV, HEAD_DIM), group3(1, HEAD_DIM), group3(1, HEAD_DIM),
                 _whole(knew_t.shape), _whole(vnew_t.shape), cache_blk, cache_blk])
    io_alias = {1: 1, 2: 2} if n_al else {}
    return pl.pallas_call(
        functools.partial(_sample_attn_kernel, layer, n_al),
        grid_spec=pltpu.PrefetchScalarGridSpec(
            num_scalar_prefetch=1, grid=(groups // S2_GROUPS,),
            in_specs=in_specs,
            out_specs=[group3(Q_PER_KV, HEAD_DIM), cache_blk, cache_blk]),
        out_shape=[jax.ShapeDtypeStruct((groups, Q_PER_KV, HEAD_DIM), F32),
                   jax.ShapeDtypeStruct(kcache.shape, F32),
                   jax.ShapeDtypeStruct(vcache.shape, F32)],
        input_output_aliases=io_alias,
        compiler_params=pltpu.CompilerParams(
            dimension_semantics=("arbitrary",), vmem_limit_bytes=V7X_VMEM_LIMIT),
        name="sample_attn",
    )(sinks, *aliased, q, knew, vnew, knew_t, vnew_t, kcache, vcache)


def _sample_out_kernel(x_ref, o_ref, mab_ref, gc_ref, gpost_ref, gpre2_ref, gpost2_ref,
                       wc_ref, wo_ref, wg_ref, wu_ref, wd_ref, y_ref):
    y_c = _dot(o_ref[...].astype(BF16), wc_ref[...])
    merged = mab_ref[...] + gc_ref[...] * y_c
    x1 = x_ref[...] + _rmsnorm(_dot(merged.astype(BF16), wo_ref[...]), gpost_ref[...])
    y_ref[...] = _swiglu(x1, gpre2_ref[...], gpost2_ref[...], wg_ref, wu_ref, wd_ref)


def _sample_out(layer, x, o, mab, gc, gpost, gpre2, gpost2, wc, wo, wg, wu, wd):
    lb = functools.partial(_layer_block, layer)
    in_specs = [_whole(x.shape), _whole(o.shape), _whole(mab.shape), _whole(gc.shape),
                lb((1, D_MODEL)), lb((1, D_MODEL)), lb((1, D_MODEL)),
                _whole((D_MODEL, D_MODEL)), _whole((D_MODEL, D_MODEL)),
                _whole((D_MODEL, D_FF)), _whole((D_MODEL, D_FF)), _whole((D_FF, D_MODEL))]
    return pl.pallas_call(
        _sample_out_kernel,
        grid=(1,),
        in_specs=in_specs,
        out_specs=pl.BlockSpec(x.shape, lambda i: (0, 0)),
        out_shape=jax.ShapeDtypeStruct(x.shape, F32),
        compiler_params=pltpu.CompilerParams(
            dimension_semantics=("arbitrary",), vmem_limit_bytes=V7X_VMEM_LIMIT),
        name="sample_out",
    )(x, o, mab, gc, gpost, gpre2, gpost2, wc, wo, wg, wu, wd)


def _rope_angles(pos):
    half = HEAD_DIM // 2
    inv = jnp.power(jnp.float32(ROPE_THETA), -jnp.arange(half, dtype=F32) * (2.0 / HEAD_DIM))
    return pos.astype(F32)[:, None] * jnp.tile(inv, LANES // half)[None, :]


def _rope_sign():
    lane = jnp.arange(LANES, dtype=jnp.int32)
    return jnp.where((lane % HEAD_DIM) < (HEAD_DIM // 2), -1.0, 1.0).astype(F32)[None, :]


def kernel(x_prompt, x_sample, state_conv, cache_win_k, cache_win_v, norm_pre_mix, norm_post_mix,
           norm_pre_ffn, norm_post_ffn, w_in, chunk_ln_g, chunk_ln_b, w_spatial, b_spatial, conv_w,
           conv_b, attn_sinks, w_br_a, w_br_b, w_br_c, w_out, w_ffn_gate, w_ffn_up, w_ffn_down):
    xp = x_prompt.reshape(SEQ, D_MODEL)
    xs = x_sample.reshape(DEC_BATCH, D_MODEL)

    sign = _rope_sign()
    ang_a = _rope_angles(jnp.arange(SEQ // TM_MIX, dtype=jnp.int32) * TM_MIX)
    ang_b = _rope_angles(jnp.arange(TM_MIX, dtype=jnp.int32))
    rope_p = (jnp.cos(ang_a)[:, None, :], jnp.sin(ang_a)[:, None, :],
              jnp.cos(ang_b), jnp.sin(ang_b), sign * jnp.cos(ang_b), sign * jnp.sin(ang_b))
    ang_s = _rope_angles(jnp.full((1,), PAST_LEN, dtype=jnp.int32))
    cos_s, sin_s = jnp.cos(ang_s), sign * jnp.sin(ang_s)

    mixer_f32 = (w_in, w_br_a, w_br_b, w_br_c, w_out)
    ffn_f32 = (w_ffn_gate, w_ffn_up, w_ffn_down)
    mixer_w = tuple(w[0].astype(BF16) for w in mixer_f32)
    rows = lambda a: a.reshape(DEPTH, 1, -1)
    gpre, gpost = rows(norm_pre_mix), rows(norm_post_mix)
    gpre2, gpost2 = rows(norm_pre_ffn), rows(norm_post_ffn)
    lng, lnb, convb = rows(chunk_ln_g), rows(chunk_ln_b), rows(conv_b)
    bsp = jnp.repeat(jnp.transpose(b_spatial, (0, 2, 1)), A_GROUP_DIM, axis=2)
    wsp0 = rows(jnp.repeat(w_spatial[:, :, 0, 0], A_GROUP_DIM, axis=1))
    bsp0 = rows(jnp.repeat(b_spatial[:, :, 0], A_GROUP_DIM, axis=1))
    hist = state_conv.reshape(DEPTH, DEC_BATCH, (CONV_W - 1) * D_MODEL)
    n_groups = DEC_BATCH * N_KV_HEADS
    to_groups = lambda c: jnp.transpose(c, (0, 1, 3, 4, 2)).reshape(DEPTH, n_groups, HEAD_DIM, WINDOW)
    kcache, vcache = to_groups(cache_win_k), to_groups(cache_win_v)

    p_conv, p_k, p_v = [], [], []
    s_conv, s_cv = [], []
    knext = vnext = None
    for l in range(DEPTH):
        w_in_b, wa, wb, wc, wo = mixer_w
        x1, cxlast, klast, vlast, wg, wu, wd = _prompt_mixer(
            l, xp, rope_p, attn_sinks, gpre, gpost, w_in_b, lng, lnb, w_spatial, bsp,
            conv_w, convb, wa, wb, wc, wo, l, ffn_f32)
        if l + 1 < DEPTH:
            xp, *mixer_w = _prompt_ffn(l, x1, gpre2, gpost2, wg, wu, wd, l + 1, mixer_f32)
        else:
            (xp,) = _prompt_ffn(l, x1, gpre2, gpost2, wg, wu, wd, 0, ())
        p_conv.append(cxlast[SUBLANES - (CONV_W - 1):].reshape(1, CONV_W - 1, D_MODEL))
        p_k.append(klast.reshape(1, WINDOW, N_KV_HEADS, HEAD_DIM))
        p_v.append(vlast.reshape(1, WINDOW, N_KV_HEADS, HEAD_DIM))

        va, newconv, q, knew, vnew, mab, gc = _sample_proj(
            l, xs, hist, cos_s, sin_s, gpre, w_in_b, lng, lnb, wsp0, bsp0, conv_w, convb, wa, wb)
        knew = knew.reshape(n_groups, HEAD_DIM)
        vnew = vnew.reshape(n_groups, HEAD_DIM)
        o, knext, vnext = _sample_attn(
            l, attn_sinks, q.reshape(n_groups, Q_PER_KV, HEAD_DIM),
            knew[:, None, :], vnew[:, None, :], jnp.transpose(knew), jnp.transpose(vnew),
            kcache, vcache, knext, vnext)
        xs = _sample_out(l, xs, o.reshape(DEC_BATCH, D_MODEL), mab, gc, gpost, gpre2, gpost2,
                         wc, wo, wg, wu, wd)
        s_conv.append(newconv.reshape(DEC_BATCH, CONV_W - 1, D_MODEL))
        s_cv.append(va.reshape(DEC_BATCH, 1, D_MODEL))

    from_groups = lambda c: jnp.transpose(
        c.reshape(DEPTH, DEC_BATCH, N_KV_HEADS, HEAD_DIM, WINDOW), (0, 1, 4, 2, 3))
    return (xp.reshape(1, SEQ, D_MODEL), xs.reshape(DEC_BATCH, 1, D_MODEL),
            jnp.stack(p_conv), jnp.stack(p_k), jnp.stack(p_v),
            jnp.stack(s_conv), from_groups(knext), from_groups(vnext), jnp.stack(s_cv))
```

```python
import functools

import jax
import jax.numpy as jnp
from jax.experimental import pallas as pl
from jax.experimental.pallas import tpu as pltpu

D_MODEL = 1024
SEQ = 16384
DEPTH = 2
DEC_BATCH = 128
PAST_LEN = 16384
CHUNK = 128
A_GROUPS = 8
A_GROUP_DIM = D_MODEL // A_GROUPS
CONV_W = 3
N_HEADS = 16
N_KV_HEADS = 4
HEAD_DIM = 64
Q_PER_KV = N_HEADS // N_KV_HEADS
WINDOW = 128
ROPE_THETA = 10000.0
D_FF = 2816
KV_DIM = N_KV_HEADS * HEAD_DIM
NEG_INF = -1e30
LOG2_E = 1.4426950408889634

OFF_UA = 0
OFF_VA = OFF_UA + D_MODEL
OFF_BG = OFF_VA + D_MODEL
OFF_CG = OFF_BG + D_MODEL
OFF_HB = OFF_CG + D_MODEL
OFF_Q = OFF_HB + D_MODEL
OFF_K = OFF_Q + N_HEADS * HEAD_DIM
OFF_V = OFF_K + KV_DIM
OFF_GA = OFF_V + KV_DIM
OFF_GB = OFF_GA + D_MODEL
OFF_GC = OFF_GB + D_MODEL
IN_WIDTH = OFF_GC + D_MODEL

LANES = 128
SUBLANES = 8
BF16_SUBLANES = 16
V7X_VMEM_LIMIT = 62 * 1024 * 1024

TM_MIX = 512
TM_FFN = 1024
FFN_ROWS = 256
BLK = WINDOW
S2_GROUPS = 64
ATTN_LOOKAHEAD = 2
GATE_COLS = 256
PIECES_BEFORE_ATTN = 3

assert CHUNK == BLK and WINDOW == BLK and PAST_LEN >= WINDOW
assert 2 * HEAD_DIM == LANES and A_GROUP_DIM == LANES
assert SEQ % TM_MIX == 0 and SEQ % TM_FFN == 0 and TM_MIX % BLK == 0
assert TM_FFN % FFN_ROWS == 0
assert LANES % S2_GROUPS == 0 and S2_GROUPS % N_KV_HEADS == 0 and WINDOW == LANES
assert (DEC_BATCH * N_KV_HEADS) % LANES == 0

BF16 = jnp.bfloat16
F32 = jnp.float32


def _dot(a, b):
    return jnp.dot(a, b, preferred_element_type=F32)


def _dot_nt(a, b):
    return jax.lax.dot_general(a, b, (((1,), (1,)), ((), ())), preferred_element_type=F32)


def _rmsnorm(x, g, eps=1e-6):
    return x * jax.lax.rsqrt(jnp.mean(x * x, axis=-1, keepdims=True) + eps) * g


def _layernorm(x, g, b, eps=1e-5):
    mu = jnp.mean(x, axis=-1, keepdims=True)
    xc = x - mu
    var = jnp.mean(xc * xc, axis=-1, keepdims=True)
    return xc * jax.lax.rsqrt(var + eps) * g + b


def _lane_iota(shape):
    return jax.lax.broadcasted_iota(jnp.int32, shape, len(shape) - 1)


def _rope_block(xb, cos_t, sin_s):
    lane = _lane_iota(xb.shape)
    first_half = (lane % HEAD_DIM) < (HEAD_DIM // 2)
    rot = jnp.where(first_half,
                    pltpu.roll(xb, LANES - HEAD_DIM // 2, axis=1),
                    pltpu.roll(xb, HEAD_DIM // 2, axis=1))
    return xb * cos_t + rot * sin_s


def _rope(x, cos_t, sin_s):
    n = x.shape[1] // LANES
    return jnp.concatenate(
        [_rope_block(x[:, j * LANES:(j + 1) * LANES], cos_t, sin_s) for j in range(n)], axis=1)


def _split_kv_heads(blk):
    out = []
    for j in range(KV_DIM // LANES):
        b = blk[:, j * LANES:(j + 1) * LANES]
        r = pltpu.roll(b, HEAD_DIM, axis=1)
        low = _lane_iota(b.shape) < HEAD_DIM
        zero = jnp.zeros_like(b)
        out.append((jnp.where(low, b, zero).astype(BF16), jnp.where(low, zero, r).astype(BF16)))
        out.append((jnp.where(low, r, zero).astype(BF16), jnp.where(low, zero, b).astype(BF16)))
    return out


def _layer_block(layer, shape):
    nd = len(shape)
    return pl.BlockSpec((None,) + tuple(shape), lambda *_: (layer,) + (0,) * nd,
                        pipeline_mode=pl.Buffered(1))


def _whole(shape):
    nd = len(shape)
    return pl.BlockSpec(tuple(shape), lambda *_: (0,) * nd, pipeline_mode=pl.Buffered(1))


N_MIXER_INPUTS = 20


def _cast_plan(rows, n_steps):
    for nb in range(n_steps, 0, -1):
        if rows % nb == 0 and (rows // nb) % BF16_SUBLANES == 0:
            return nb
    raise ValueError(f"no bf16-aligned row split of {rows} rows over {n_steps} steps")


def _cast_specs(layer, w_stacked, n_steps):
    _, rows, cols = w_stacked.shape
    nb = _cast_plan(rows, n_steps)
    blk = lambda i: jnp.minimum(i, nb - 1)
    return (pl.BlockSpec((None, rows // nb, cols), lambda i, *_: (layer, blk(i), 0)),
            pl.BlockSpec((rows // nb, cols), lambda i, *_: (blk(i), 0)),
            jax.ShapeDtypeStruct((rows, cols), BF16))


def _cast_row_blocks(src_refs, dst_refs):
    for src, dst in zip(src_refs, dst_refs):
        dst[...] = src[...].astype(BF16)


def _mixer_kernel(layer, n_cast, sinks_ref, *refs):
    (x_ref, cos_a_ref, sin_a_ref, cos_b_ref, sin_b_ref, cos_bs_ref, sin_bs_ref,
     gpre_ref, gpost_ref, w_in_ref, lng_ref, lnb_ref, wsp_ref, bsp_ref, convw_ref, convb_ref,
     wa_ref, wb_ref, wc_ref, wo_ref) = refs[:N_MIXER_INPUTS]
    cast_src = refs[N_MIXER_INPUTS:N_MIXER_INPUTS + n_cast]
    outs = refs[N_MIXER_INPUTS + n_cast:]
    x1_ref, cxlast_ref, klast_ref, vlast_ref = outs[:4]
    cast_dst = outs[4:4 + n_cast]
    wsp_s, kprev_s, vprev_s, cxprev_s = outs[4 + n_cast:]
    step = pl.program_id(0)
    nblk = TM_MIX // BLK

    @pl.when(step == 0)
    def _init():
        row = jax.lax.broadcasted_iota(jnp.int32, (CHUNK, CHUNK), 0)
        col = jax.lax.broadcasted_iota(jnp.int32, (CHUNK, CHUNK), 1)
        for g in range(A_GROUPS):
            wsp_s[g] = jnp.where(col <= row, wsp_ref[g], 0.0).astype(BF16)
        kprev_s[...] = jnp.zeros_like(kprev_s)
        vprev_s[...] = jnp.zeros_like(vprev_s)
        cxprev_s[...] = jnp.zeros_like(cxprev_s)

    x = x_ref[...]
    h = _rmsnorm(x, gpre_ref[...]).astype(BF16)

    def proj(off, width):
        return _dot(h, w_in_ref[:, off:off + width])

    va = _layernorm(jax.nn.gelu(proj(OFF_VA, D_MODEL)), lng_ref[...], lnb_ref[...]).astype(BF16)
    ua = jax.nn.gelu(proj(OFF_UA, D_MODEL))
    cx = proj(OFF_CG, D_MODEL) * proj(OFF_HB, D_MODEL)
    _cast_row_blocks(cast_src, cast_dst)
    prev = cxprev_s[...]
    rowi = jax.lax.broadcasted_iota(jnp.int32, cx.shape, 0)
    cx1 = jnp.where(rowi < 1, prev[SUBLANES - 1:SUBLANES, :], pltpu.roll(cx, 1, axis=0))
    cx2 = jnp.where(rowi < 2,
                    jnp.where(rowi < 1, prev[SUBLANES - 2:SUBLANES - 1, :],
                              prev[SUBLANES - 1:SUBLANES, :]),
                    pltpu.roll(cx, 2, axis=0))
    conv = convb_ref[...] + convw_ref[0:1, :] * cx2
    conv = conv + convw_ref[1:2, :] * cx1
    conv = conv + convw_ref[2:3, :] * cx
    cxprev_s[...] = cx[TM_MIX - SUBLANES:, :]
    cxlast_ref[...] = cx[TM_MIX - SUBLANES:, :]
    bx = (proj(OFF_BG, D_MODEL) * conv).astype(BF16)
    sa_rows = []
    for c in range(nblk):
        sa_rows.append(jnp.concatenate(
            [_dot(wsp_s[g], va[c * BLK:(c + 1) * BLK, g * LANES:(g + 1) * LANES])
             for g in range(A_GROUPS)], axis=1) + bsp_ref[...])
    ax = (ua * jnp.concatenate(sa_rows, axis=0)).astype(BF16)

    cos_a, sin_a = cos_a_ref[...], sin_a_ref[...]
    cos_t = cos_a * cos_b_ref[...] - sin_a * sin_b_ref[...]
    sin_s = sin_a * cos_bs_ref[...] + cos_a * sin_bs_ref[...]
    q = (_rope(proj(OFF_Q, N_HEADS * HEAD_DIM), cos_t, sin_s)
         * (LOG2_E * HEAD_DIM ** -0.5)).astype(BF16)
    k = _rope(proj(OFF_K, KV_DIM), cos_t, sin_s)
    v = proj(OFF_V, KV_DIM)
    klast_ref[...] = k[TM_MIX - BLK:, :]
    vlast_ref[...] = v[TM_MIX - BLK:, :]

    merged_ab, gate_c = [], []

    def gated_branch_pieces():
        for j in range(D_MODEL // GATE_COLS):
            cols = slice(j * GATE_COLS, (j + 1) * GATE_COLS)
            y_a = _dot(ax, wa_ref[:, cols])
            yield
            part = jax.nn.sigmoid(proj(OFF_GA + j * GATE_COLS, GATE_COLS)) * y_a
            yield
            y_b = _dot(bx, wb_ref[:, cols])
            yield
            merged_ab.append(part + jax.nn.sigmoid(proj(OFF_GB + j * GATE_COLS, GATE_COLS)) * y_b)
            yield
            gate_c.append(jax.nn.sigmoid(proj(OFF_GC + j * GATE_COLS, GATE_COLS)))
            yield

    pieces = gated_branch_pieces()
    n_pieces = 5 * (D_MODEL // GATE_COLS)
    for _ in range(PIECES_BEFORE_ATTN):
        next(pieces)

    qi = jax.lax.broadcasted_iota(jnp.int32, (BLK, 2 * BLK), 0)
    kj = jax.lax.broadcasted_iota(jnp.int32, (BLK, 2 * BLK), 1)
    band = (kj >= qi) & (kj <= qi + WINDOW)
    lane_low = _lane_iota((BLK, LANES)) < HEAD_DIM

    k_parts = [[(kprev_s[g, 0], kprev_s[g, 1]) for g in range(N_KV_HEADS)]]
    v_parts = [[(vprev_s[g, 0], vprev_s[g, 1]) for g in range(N_KV_HEADS)]]
    for c in range(nblk):
        k_parts.append(_split_kv_heads(k[c * BLK:(c + 1) * BLK, :]))
        v_parts.append(_split_kv_heads(v[c * BLK:(c + 1) * BLK, :]))
    for g in range(N_KV_HEADS):
        kprev_s[g, 0] = k_parts[nblk][g][0]
        kprev_s[g, 1] = k_parts[nblk][g][1]
        vprev_s[g, 0] = v_parts[nblk][g][0]
        vprev_s[g, 1] = v_parts[nblk][g][1]
    mask_first = band & (kj >= jnp.where(step == 0, BLK, 0))

    def block_diag(parts, c, g):
        return jnp.concatenate([parts[c][g][0], parts[c + 1][g][0],
                                parts[c][g][1], parts[c + 1][g][1]], axis=0)

    def scores(c, m):
        kbd = block_diag(k_parts, c, (2 * m) // Q_PER_KV)
        return _dot_nt(q[c * BLK:(c + 1) * BLK, m * LANES:(m + 1) * LANES], kbd)

    def softmax_numerators(c, m, s):
        mask = mask_first if c == 0 else band
        es, invs = [], []
        for hh in range(2):
            sink = sinks_ref[layer, 2 * m + hh] * LOG2_E
            sh = jnp.where(mask, s[:, hh * 2 * BLK:(hh + 1) * 2 * BLK], NEG_INF)
            mx = jnp.maximum(jnp.max(sh, axis=-1, keepdims=True), sink)
            e = jnp.exp2(sh - mx)
            den = jnp.sum(e, axis=-1, keepdims=True) + jnp.exp2(sink - mx)
            es.append(e.astype(BF16))
            invs.append(1.0 / den)
        return jnp.concatenate(es, axis=1), jnp.where(lane_low, invs[0], invs[1])

    def weighted_values(c, m, e, inv):
        vbd = block_diag(v_parts, c, (2 * m) // Q_PER_KV)
        return _dot(e, vbd) * inv

    its = [(c, m) for c in range(nblk) for m in range(N_HEADS // 2)]
    s_queue = [scores(*its[i]) for i in range(min(ATTN_LOOKAHEAD, len(its)))]
    o_parts = {}
    issued = PIECES_BEFORE_ATTN
    for i, (c, m) in enumerate(its):
        if i + ATTN_LOOKAHEAD < len(its):
            s_queue.append(scores(*its[i + ATTN_LOOKAHEAD]))
        e, inv = softmax_numerators(c, m, s_queue.pop(0))
        o_parts[(c, m)] = weighted_values(c, m, e, inv)
        while issued < PIECES_BEFORE_ATTN + ((i + 1) * (n_pieces - PIECES_BEFORE_ATTN)) // len(its):
            next(pieces)
            issued += 1
    o = jnp.concatenate(
        [jnp.concatenate([o_parts[(c, m)] for m in range(N_HEADS // 2)], axis=1)
         for c in range(nblk)], axis=0).astype(BF16)
    y_c = _dot(o, wc_ref[...])
    merged = jnp.concatenate(merged_ab, axis=1) + jnp.concatenate(gate_c, axis=1) * y_c

    merged = merged.astype(BF16)
    for c in range(nblk):
        rows = slice(c * BLK, (c + 1) * BLK)
        x1_ref[rows, :] = x[rows, :] + _rmsnorm(_dot(merged[rows, :], wo_ref[...]), gpost_ref[...])


def _prompt_mixer(layer, x, rope, sinks, gpre, gpost, w_in, lng, lnb, wsp, bsp, convw, convb,
                  wa, wb, wc, wo, cast_layer, cast_weights):
    n = x.shape[0]
    n_steps = n // TM_MIX
    lb = functools.partial(_layer_block, layer)
    casts = [_cast_specs(cast_layer, w, n_steps) for w in cast_weights]
    in_specs = [
        pl.BlockSpec((TM_MIX, D_MODEL), lambda i, *_: (i, 0)),
        pl.BlockSpec((None, 1, LANES), lambda i, *_: (i, 0, 0)),
        pl.BlockSpec((None, 1, LANES), lambda i, *_: (i, 0, 0)),
        _whole((TM_MIX, LANES)), _whole((TM_MIX, LANES)),
        _whole((TM_MIX, LANES)), _whole((TM_MIX, LANES)),
        lb((1, D_MODEL)), lb((1, D_MODEL)),
        _whole((D_MODEL, IN_WIDTH)),
        lb((1, D_MODEL)), lb((1, D_MODEL)),
        lb((A_GROUPS, CHUNK, CHUNK)), lb((CHUNK, D_MODEL)),
        lb((CONV_W, D_MODEL)), lb((1, D_MODEL)),
        _whole((D_MODEL, D_MODEL)), _whole((D_MODEL, D_MODEL)),
        _whole((D_MODEL, D_MODEL)), _whole((D_MODEL, D_MODEL)),
    ] + [c[0] for c in casts]
    assert len(in_specs) == N_MIXER_INPUTS + len(casts)
    out_specs = [
        pl.BlockSpec((TM_MIX, D_MODEL), lambda i, *_: (i, 0)),
        pl.BlockSpec((SUBLANES, D_MODEL), lambda i, *_: (0, 0)),
        pl.BlockSpec((BLK, KV_DIM), lambda i, *_: (0, 0)),
        pl.BlockSpec((BLK, KV_DIM), lambda i, *_: (0, 0)),
    ] + [c[1] for c in casts]
    out_shape = [
        jax.ShapeDtypeStruct((n, D_MODEL), F32),
        jax.ShapeDtypeStruct((SUBLANES, D_MODEL), F32),
        jax.ShapeDtypeStruct((BLK, KV_DIM), F32),
        jax.ShapeDtypeStruct((BLK, KV_DIM), F32),
    ] + [c[2] for c in casts]
    scratch = [
        pltpu.VMEM((A_GROUPS, CHUNK, CHUNK), BF16),
        pltpu.VMEM((N_KV_HEADS, 2, BLK, LANES), BF16),
        pltpu.VMEM((N_KV_HEADS, 2, BLK, LANES), BF16),
        pltpu.VMEM((SUBLANES, D_MODEL), F32),
    ]
    return pl.pallas_call(
        functools.partial(_mixer_kernel, layer, len(casts)),
        grid_spec=pltpu.PrefetchScalarGridSpec(
            num_scalar_prefetch=1, grid=(n_steps,), in_specs=in_specs, out_specs=out_specs,
            scratch_shapes=scratch),
        out_shape=out_shape,
        compiler_params=pltpu.CompilerParams(
            dimension_semantics=("arbitrary",), vmem_limit_bytes=V7X_VMEM_LIMIT),
        name="prompt_mixer",
    )(sinks, x, *rope, gpre, gpost, w_in, lng, lnb, wsp, bsp, convw, convb, wa, wb, wc, wo,
      *cast_weights)


def _swiglu(x, gpre, gpost, wg_ref, wu_ref, wd_ref):
    h = _rmsnorm(x, gpre).astype(BF16)
    a = jax.nn.silu(_dot(h, wg_ref[...])) * _dot(h, wu_ref[...])
    return x + _rmsnorm(_dot(a.astype(BF16), wd_ref[...]), gpost)


N_FFN_INPUTS = 13


def _ffn_kernel(n_cast, *refs):
    (x_ref, gpre_ref, gpost_ref, wg_ref, wu_ref, wd_ref,
     xs_ref, os_ref, mab_ref, gc_ref, gpost_mix_ref, wc_ref, wo_ref) = refs[:N_FFN_INPUTS]
    cast_src = refs[N_FFN_INPUTS:N_FFN_INPUTS + n_cast]
    y_ref, ys_ref = refs[N_FFN_INPUTS + n_cast:N_FFN_INPUTS + n_cast + 2]
    cast_dst = refs[N_FFN_INPUTS + n_cast + 2:]
    rows = [slice(r * FFN_ROWS, (r + 1) * FFN_ROWS) for r in range(TM_FFN // FFN_ROWS)]
    xs = [x_ref[r, :] for r in rows]
    acts = []
    for x in xs:
        h = _rmsnorm(x, gpre_ref[...]).astype(BF16)
        acts.append((jax.nn.silu(_dot(h, wg_ref[...])) * _dot(h, wu_ref[...])).astype(BF16))
    for r, x, a in zip(rows, xs, acts):
        y_ref[r, :] = x + _rmsnorm(_dot(a, wd_ref[...]), gpost_ref[...])
    _cast_row_blocks(cast_src, cast_dst)

    @pl.when(pl.program_id(0) == pl.num_programs(0) - 1)
    def _sample_rows():
        y_c = _dot(os_ref[...].astype(BF16), wc_ref[...])
        merged = mab_ref[...] + gc_ref[...] * y_c
        x1 = xs_ref[...] + _rmsnorm(_dot(merged.astype(BF16), wo_ref[...]), gpost_mix_ref[...])
        ys_ref[...] = _swiglu(x1, gpre_ref[...], gpost_ref[...], wg_ref, wu_ref, wd_ref)


def _ffn(layer, x, gpre, gpost, wg, wu, wd, xs, o_s, mab, gc, gpost_mix, wc, wo,
         cast_layer, cast_weights):
    n = x.shape[0]
    n_steps = n // TM_FFN
    lb = functools.partial(_layer_block, layer)
    casts = [_cast_specs(cast_layer, w, n_steps) for w in cast_weights]
    row_tile = pl.BlockSpec((TM_FFN, D_MODEL), lambda i: (i, 0))
    sample_rows = _whole(xs.shape)
    return pl.pallas_call(
        functools.partial(_ffn_kernel, len(casts)),
        grid=(n_steps,),
        in_specs=[row_tile, lb((1, D_MODEL)), lb((1, D_MODEL)),
                  _whole((D_MODEL, D_FF)), _whole((D_MODEL, D_FF)), _whole((D_FF, D_MODEL)),
                  sample_rows, sample_rows, sample_rows, sample_rows, lb((1, D_MODEL)),
                  _whole((D_MODEL, D_MODEL)), _whole((D_MODEL, D_MODEL))]
                 + [c[0] for c in casts],
        out_specs=[row_tile, pl.BlockSpec(xs.shape, lambda i: (0, 0))] + [c[1] for c in casts],
        out_shape=[jax.ShapeDtypeStruct((n, D_MODEL), F32), jax.ShapeDtypeStruct(xs.shape, F32)]
                  + [c[2] for c in casts],
        compiler_params=pltpu.CompilerParams(
            dimension_semantics=("arbitrary",), vmem_limit_bytes=V7X_VMEM_LIMIT),
        name="ffn",
    )(x, gpre, gpost, wg, wu, wd, xs, o_s, mab, gc, gpost_mix, wc, wo, *cast_weights)


def _sample_proj_kernel(x_ref, hist_ref, cos_ref, sin_ref, gpre_ref, w_in_ref, lng_ref, lnb_ref,
                        wsp0_ref, bsp0_ref, convw_ref, convb_ref, wa_ref, wb_ref,
                        va_ref, newconv_ref, q_ref, k_ref, v_ref, mab_ref, gc_ref):
    x = x_ref[...]
    h = _rmsnorm(x, gpre_ref[...]).astype(BF16)

    def proj(off, width):
        return _dot(h, w_in_ref[:, off:off + width])

    ua = jax.nn.gelu(proj(OFF_UA, D_MODEL))
    va = _layernorm(jax.nn.gelu(proj(OFF_VA, D_MODEL)), lng_ref[...], lnb_ref[...])
    va_ref[...] = va
    sa = wsp0_ref[...] * va + bsp0_ref[...]
    y_a = _dot((ua * sa).astype(BF16), wa_ref[...])
    merged = jax.nn.sigmoid(proj(OFF_GA, D_MODEL)) * y_a

    cx = proj(OFF_CG, D_MODEL) * proj(OFF_HB, D_MODEL)
    h0 = hist_ref[:, 0:D_MODEL]
    h1 = hist_ref[:, D_MODEL:2 * D_MODEL]
    conv = convb_ref[...] + convw_ref[0:1, :] * h0
    conv = conv + convw_ref[1:2, :] * h1
    conv = conv + convw_ref[2:3, :] * cx
    newconv_ref[:, 0:D_MODEL] = h1
    newconv_ref[:, D_MODEL:2 * D_MODEL] = cx
    y_b = _dot((proj(OFF_BG, D_MODEL) * conv).astype(BF16), wb_ref[...])
    mab_ref[...] = merged + jax.nn.sigmoid(proj(OFF_GB, D_MODEL)) * y_b
    gc_ref[...] = jax.nn.sigmoid(proj(OFF_GC, D_MODEL))

    cos_t = cos_ref[...]
    sin_s = sin_ref[...]
    q_ref[...] = _rope(proj(OFF_Q, N_HEADS * HEAD_DIM), cos_t, sin_s) * (HEAD_DIM ** -0.5)
    k_ref[...] = _rope(proj(OFF_K, KV_DIM), cos_t, sin_s)
    v_ref[...] = proj(OFF_V, KV_DIM)


def _sample_proj(layer, x, hist, cos_t, sin_s, gpre, w_in, lng, lnb, wsp0, bsp0, convw, convb,
                 wa, wb):
    b = x.shape[0]
    lb = functools.partial(_layer_block, layer)
    in_specs = [
        _whole(x.shape), lb((b, (CONV_W - 1) * D_MODEL)), _whole(cos_t.shape), _whole(sin_s.shape),
        lb((1, D_MODEL)), _whole((D_MODEL, IN_WIDTH)), lb((1, D_MODEL)), lb((1, D_MODEL)),
        lb((1, D_MODEL)), lb((1, D_MODEL)), lb((CONV_W, D_MODEL)), lb((1, D_MODEL)),
        _whole((D_MODEL, D_MODEL)), _whole((D_MODEL, D_MODEL)),
    ]
    out_shape = [
        jax.ShapeDtypeStruct((b, D_MODEL), F32),
        jax.ShapeDtypeStruct((b, 2 * D_MODEL), F32),
        jax.ShapeDtypeStruct((b, D_MODEL), F32),
        jax.ShapeDtypeStruct((b, KV_DIM), F32),
        jax.ShapeDtypeStruct((b, KV_DIM), F32),
        jax.ShapeDtypeStruct((b, D_MODEL), F32),
        jax.ShapeDtypeStruct((b, D_MODEL), F32),
    ]
    return pl.pallas_call(
        _sample_proj_kernel,
        grid=(1,),
        in_specs=in_specs,
        out_specs=[pl.BlockSpec(s.shape, lambda i: (0, 0)) for s in out_shape],
        out_shape=out_shape,
        compiler_params=pltpu.CompilerParams(
            dimension_semantics=("arbitrary",), vmem_limit_bytes=V7X_VMEM_LIMIT),
        name="sample_proj",
    )(x, hist, cos_t, sin_s, gpre, w_in, lng, lnb, wsp0, bsp0, convw, convb, wa, wb)


def _sample_attn_kernel(layer, n_aliased, sinks_ref, *refs):
    (q_ref, knew_ref, vnew_ref, knew_t_ref, vnew_t_ref, kc_ref, vc_ref,
     o_ref, kout_ref, vout_ref) = refs[n_aliased:]
    step = pl.program_id(0)
    qf = q_ref[...]
    s = jnp.einsum('grd,gdk->grk', qf.astype(BF16), kc_ref[...].astype(BF16),
                   preferred_element_type=F32)
    s_new = jnp.sum(qf * knew_ref[...], axis=-1, keepdims=True)
    gidx = jax.lax.broadcasted_iota(jnp.int32, (S2_GROUPS, Q_PER_KV, 1), 0)
    ridx = jax.lax.broadcasted_iota(jnp.int32, (S2_GROUPS, Q_PER_KV, 1), 1)
    head = (gidx % N_KV_HEADS) * Q_PER_KV + ridx
    sink = jnp.zeros((S2_GROUPS, Q_PER_KV, 1), F32)
    for hh in range(N_HEADS):
        sink = jnp.where(head == hh, sinks_ref[layer, hh], sink)
    mx = jnp.maximum(jnp.maximum(jnp.max(s, axis=-1, keepdims=True), s_new), sink)
    e = jnp.exp(s - mx)
    e_new = jnp.exp(s_new - mx)
    den = jnp.sum(e, axis=-1, keepdims=True) + e_new + jnp.exp(sink - mx)
    o = jnp.einsum('grk,gdk->grd', e.astype(BF16), vc_ref[...].astype(BF16),
                   preferred_element_type=F32)
    o_ref[...] = (o + e_new * vnew_ref[...]) * (1.0 / den)

    steps_per_tile = LANES // S2_GROUPS
    tile = pl.multiple_of((step // steps_per_tile) * LANES, LANES)
    first_col = (step % steps_per_tile) * S2_GROUPS
    last_lane = _lane_iota((HEAD_DIM, WINDOW)) == WINDOW - 1
    for new_t_ref, cache_ref, out_ref in ((knew_t_ref, kc_ref, kout_ref),
                                          (vnew_t_ref, vc_ref, vout_ref)):
        new_cols = new_t_ref[:, pl.ds(tile, LANES)]
        for j in range(S2_GROUPS):
            col = pltpu.roll(new_cols, WINDOW - 1 - first_col - j, axis=1)
            out_ref[j] = jnp.where(last_lane, col, pltpu.roll(cache_ref[j], WINDOW - 1, axis=1))


def _sample_attn(layer, sinks, q, knew, vnew, knew_t, vnew_t, kcache, vcache, kout_prev, vout_prev):
    groups = q.shape[0]
    group3 = lambda d1, d2: pl.BlockSpec((S2_GROUPS, d1, d2), lambda i, *_: (i, 0, 0))
    cache_blk = pl.BlockSpec((None, S2_GROUPS, HEAD_DIM, WINDOW), lambda i, *_: (layer, i, 0, 0))
    aliased = [] if kout_prev is None else [kout_prev, vout_prev]
    n_al = len(aliased)
    in_specs = ([pl.BlockSpec(memory_space=pl.ANY)] * n_al +
                [group3(Q_PER_KV, HEAD_DIM), group3(1, HEAD_DIM), group3(1, HEAD_DIM),
                 _whole(knew_t.shape), _whole(vnew_t.shape), cache_blk, cache_blk])
    io_alias = {1: 1, 2: 2} if n_al else {}
    return pl.pallas_call(
        functools.partial(_sample_attn_kernel, layer, n_al),
        grid_spec=pltpu.PrefetchScalarGridSpec(
            num_scalar_prefetch=1, grid=(groups // S2_GROUPS,),
            in_specs=in_specs,
            out_specs=[group3(Q_PER_KV, HEAD_DIM), cache_blk, cache_blk]),
        out_shape=[jax.ShapeDtypeStruct((groups, Q_PER_KV, HEAD_DIM), F32),
                   jax.ShapeDtypeStruct(kcache.shape, F32),
                   jax.ShapeDtypeStruct(vcache.shape, F32)],
        input_output_aliases=io_alias,
        compiler_params=pltpu.CompilerParams(
            dimension_semantics=("arbitrary",), vmem_limit_bytes=V7X_VMEM_LIMIT),
        name="sample_attn",
    )(sinks, *aliased, q, knew, vnew, knew_t, vnew_t, kcache, vcache)


def _rope_angles(pos):
    half = HEAD_DIM // 2
    inv = jnp.power(jnp.float32(ROPE_THETA), -jnp.arange(half, dtype=F32) * (2.0 / HEAD_DIM))
    return pos.astype(F32)[:, None] * jnp.tile(inv, LANES // half)[None, :]


def _rope_sign():
    lane = jnp.arange(LANES, dtype=jnp.int32)
    return jnp.where((lane % HEAD_DIM) < (HEAD_DIM // 2), -1.0, 1.0).astype(F32)[None, :]


def kernel(x_prompt, x_sample, state_conv, cache_win_k, cache_win_v, norm_pre_mix, norm_post_mix,
           norm_pre_ffn, norm_post_ffn, w_in, chunk_ln_g, chunk_ln_b, w_spatial, b_spatial, conv_w,
           conv_b, attn_sinks, w_br_a, w_br_b, w_br_c, w_out, w_ffn_gate, w_ffn_up, w_ffn_down):
    xp = x_prompt.reshape(SEQ, D_MODEL)
    xs = x_sample.reshape(DEC_BATCH, D_MODEL)

    sign = _rope_sign()
    ang_a = _rope_angles(jnp.arange(SEQ // TM_MIX, dtype=jnp.int32) * TM_MIX)
    ang_b = _rope_angles(jnp.arange(TM_MIX, dtype=jnp.int32))
    rope_p = (jnp.cos(ang_a)[:, None, :], jnp.sin(ang_a)[:, None, :],
              jnp.cos(ang_b), jnp.sin(ang_b), sign * jnp.cos(ang_b), sign * jnp.sin(ang_b))
    ang_s = _rope_angles(jnp.full((1,), PAST_LEN, dtype=jnp.int32))
    cos_s, sin_s = jnp.cos(ang_s), sign * jnp.sin(ang_s)

    mixer_f32 = (w_in, w_br_a, w_br_b, w_br_c, w_out)
    ffn_f32 = (w_ffn_gate, w_ffn_up, w_ffn_down)
    mixer_w = tuple(w[0].astype(BF16) for w in mixer_f32)
    rows = lambda a: a.reshape(DEPTH, 1, -1)
    gpre, gpost = rows(norm_pre_mix), rows(norm_post_mix)
    gpre2, gpost2 = rows(norm_pre_ffn), rows(norm_post_ffn)
    lng, lnb, convb = rows(chunk_ln_g), rows(chunk_ln_b), rows(conv_b)
    bsp = jnp.repeat(jnp.transpose(b_spatial, (0, 2, 1)), A_GROUP_DIM, axis=2)
    wsp0 = rows(jnp.repeat(w_spatial[:, :, 0, 0], A_GROUP_DIM, axis=1))
    bsp0 = rows(jnp.repeat(b_spatial[:, :, 0], A_GROUP_DIM, axis=1))
    hist = state_conv.reshape(DEPTH, DEC_BATCH, (CONV_W - 1) * D_MODEL)
    n_groups = DEC_BATCH * N_KV_HEADS
    to_groups = lambda c: jnp.transpose(c, (0, 1, 3, 4, 2)).reshape(DEPTH, n_groups, HEAD_DIM, WINDOW)
    kcache, vcache = to_groups(cache_win_k), to_groups(cache_win_v)

    p_conv, p_k, p_v = [], [], []
    s_conv, s_cv = [], []
    knext = vnext = None
    for l in range(DEPTH):
        w_in_b, wa, wb, wc, wo = mixer_w
        x1, cxlast, klast, vlast, wg, wu, wd = _prompt_mixer(
            l, xp, rope_p, attn_sinks, gpre, gpost, w_in_b, lng, lnb, w_spatial, bsp,
            conv_w, convb, wa, wb, wc, wo, l, ffn_f32)
        p_conv.append(cxlast[SUBLANES - (CONV_W - 1):].reshape(1, CONV_W - 1, D_MODEL))
        p_k.append(klast.reshape(1, WINDOW, N_KV_HEADS, HEAD_DIM))
        p_v.append(vlast.reshape(1, WINDOW, N_KV_HEADS, HEAD_DIM))

        va, newconv, q, knew, vnew, mab, gc = _sample_proj(
            l, xs, hist, cos_s, sin_s, gpre, w_in_b, lng, lnb, wsp0, bsp0, conv_w, convb, wa, wb)
        knew = knew.reshape(n_groups, HEAD_DIM)
        vnew = vnew.reshape(n_groups, HEAD_DIM)
        o, knext, vnext = _sample_attn(
            l, attn_sinks, q.reshape(n_groups, Q_PER_KV, HEAD_DIM),
            knew[:, None, :], vnew[:, None, :], jnp.transpose(knew), jnp.transpose(vnew),
            kcache, vcache, knext, vnext)
        s_conv.append(newconv.reshape(DEC_BATCH, CONV_W - 1, D_MODEL))
        s_cv.append(va.reshape(DEC_BATCH, 1, D_MODEL))

        cast_next = (l + 1, mixer_f32) if l + 1 < DEPTH else (0, ())
        xp, xs, *mixer_w = _ffn(l, x1, gpre2, gpost2, wg, wu, wd,
                                xs, o.reshape(DEC_BATCH, D_MODEL), mab, gc, gpost, wc, wo,
                                *cast_next)

    from_groups = lambda c: jnp.transpose(
        c.reshape(DEPTH, DEC_BATCH, N_KV_HEADS, HEAD_DIM, WINDOW), (0, 1, 4, 2, 3))
    return (xp.reshape(1, SEQ, D_MODEL), xs.reshape(DEC_BATCH, 1, D_MODEL),
            jnp.stack(p_conv), jnp.stack(p_k), jnp.stack(p_v),
            jnp.stack(s_conv), from_groups(knext), from_groups(vnext), jnp.stack(s_cv))
```

```python
import functools

import jax
import jax.numpy as jnp
from jax.experimental import pallas as pl
from jax.experimental.pallas import tpu as pltpu

D_MODEL = 1024
SEQ = 16384
DEPTH = 2
DEC_BATCH = 128
PAST_LEN = 16384
CHUNK = 128
A_GROUPS = 8
A_GROUP_DIM = D_MODEL // A_GROUPS
CONV_W = 3
N_HEADS = 16
N_KV_HEADS = 4
HEAD_DIM = 64
Q_PER_KV = N_HEADS // N_KV_HEADS
WINDOW = 128
ROPE_THETA = 10000.0
D_FF = 2816
KV_DIM = N_KV_HEADS * HEAD_DIM
NEG_INF = -1e30
LOG2_E = 1.4426950408889634

OFF_UA = 0
OFF_VA = OFF_UA + D_MODEL
OFF_BG = OFF_VA + D_MODEL
OFF_CG = OFF_BG + D_MODEL
OFF_HB = OFF_CG + D_MODEL
OFF_Q = OFF_HB + D_MODEL
OFF_K = OFF_Q + N_HEADS * HEAD_DIM
OFF_V = OFF_K + KV_DIM
OFF_GA = OFF_V + KV_DIM
OFF_GB = OFF_GA + D_MODEL
OFF_GC = OFF_GB + D_MODEL
IN_WIDTH = OFF_GC + D_MODEL

LANES = 128
SUBLANES = 8
BF16_SUBLANES = 16
V7X_VMEM_LIMIT = 62 * 1024 * 1024

TM_MIX = 512
TM_FFN = 1024
FFN_ROWS = 256
BLK = WINDOW
S2_GROUPS = 64
ATTN_LOOKAHEAD = 2
GATE_COLS = 256
PIECES_BEFORE_ATTN = 3

assert CHUNK == BLK and WINDOW == BLK and PAST_LEN >= WINDOW
assert 2 * HEAD_DIM == LANES and A_GROUP_DIM == LANES
assert SEQ % TM_MIX == 0 and SEQ % TM_FFN == 0 and TM_MIX % BLK == 0
assert TM_FFN % FFN_ROWS == 0
assert LANES % S2_GROUPS == 0 and S2_GROUPS % N_KV_HEADS == 0 and WINDOW == LANES
assert (DEC_BATCH * N_KV_HEADS) % LANES == 0

BF16 = jnp.bfloat16
F32 = jnp.float32


def _dot(a, b):
    return jnp.dot(a, b, preferred_element_type=F32)


def _dot_nt(a, b):
    return jax.lax.dot_general(a, b, (((1,), (1,)), ((), ())), preferred_element_type=F32)


def _rmsnorm(x, g, eps=1e-6):
    return x * jax.lax.rsqrt(jnp.mean(x * x, axis=-1, keepdims=True) + eps) * g


def _layernorm(x, g, b, eps=1e-5):
    mu = jnp.mean(x, axis=-1, keepdims=True)
    xc = x - mu
    var = jnp.mean(xc * xc, axis=-1, keepdims=True)
    return xc * jax.lax.rsqrt(var + eps) * g + b


def _lane_iota(shape):
    return jax.lax.broadcasted_iota(jnp.int32, shape, len(shape) - 1)


def _rope_block(xb, cos_t, sin_s):
    lane = _lane_iota(xb.shape)
    first_half = (lane % HEAD_DIM) < (HEAD_DIM // 2)
    rot = jnp.where(first_half,
                    pltpu.roll(xb, LANES - HEAD_DIM // 2, axis=1),
                    pltpu.roll(xb, HEAD_DIM // 2, axis=1))
    return xb * cos_t + rot * sin_s


def _rope(x, cos_t, sin_s):
    n = x.shape[1] // LANES
    return jnp.concatenate(
        [_rope_block(x[:, j * LANES:(j + 1) * LANES], cos_t, sin_s) for j in range(n)], axis=1)


def _split_kv_heads(blk):
    out = []
    for j in range(KV_DIM // LANES):
        b = blk[:, j * LANES:(j + 1) * LANES]
        r = pltpu.roll(b, HEAD_DIM, axis=1)
        low = _lane_iota(b.shape) < HEAD_DIM
        zero = jnp.zeros_like(b)
        out.append((jnp.where(low, b, zero).astype(BF16), jnp.where(low, zero, r).astype(BF16)))
        out.append((jnp.where(low, r, zero).astype(BF16), jnp.where(low, zero, b).astype(BF16)))
    return out


def _layer_block(layer, shape):
    nd = len(shape)
    return pl.BlockSpec((None,) + tuple(shape), lambda *_: (layer,) + (0,) * nd,
                        pipeline_mode=pl.Buffered(1))


def _whole(shape):
    nd = len(shape)
    return pl.BlockSpec(tuple(shape), lambda *_: (0,) * nd, pipeline_mode=pl.Buffered(1))


N_MIXER_INPUTS = 20


def _cast_plan(rows, n_steps):
    for nb in range(n_steps, 0, -1):
        if rows % nb == 0 and (rows // nb) % BF16_SUBLANES == 0:
            return nb
    raise ValueError(f"no bf16-aligned row split of {rows} rows over {n_steps} steps")


def _cast_specs(layer, w_stacked, n_steps):
    _, rows, cols = w_stacked.shape
    nb = _cast_plan(rows, n_steps)
    blk = lambda i: jnp.minimum(i, nb - 1)
    return (pl.BlockSpec((None, rows // nb, cols), lambda i, *_: (layer, blk(i), 0)),
            pl.BlockSpec((rows // nb, cols), lambda i, *_: (blk(i), 0)),
            jax.ShapeDtypeStruct((rows, cols), BF16))


def _cast_row_blocks(src_refs, dst_refs):
    for src, dst in zip(src_refs, dst_refs):
        dst[...] = src[...].astype(BF16)


def _mixer_kernel(layer, n_cast, sinks_ref, *refs):
    (x_ref, cos_a_ref, sin_a_ref, cos_b_ref, sin_b_ref, cos_bs_ref, sin_bs_ref,
     gpre_ref, gpost_ref, w_in_ref, lng_ref, lnb_ref, wsp_ref, bsp_ref, convw_ref, convb_ref,
     wa_ref, wb_ref, wc_ref, wo_ref) = refs[:N_MIXER_INPUTS]
    cast_src = refs[N_MIXER_INPUTS:N_MIXER_INPUTS + n_cast]
    outs = refs[N_MIXER_INPUTS + n_cast:]
    x1_ref, cxlast_ref, klast_ref, vlast_ref = outs[:4]
    cast_dst = outs[4:4 + n_cast]
    wsp_s, kprev_s, vprev_s, cxprev_s = outs[4 + n_cast:]
    step = pl.program_id(0)
    nblk = TM_MIX // BLK

    @pl.when(step == 0)
    def _init():
        row = jax.lax.broadcasted_iota(jnp.int32, (CHUNK, CHUNK), 0)
        col = jax.lax.broadcasted_iota(jnp.int32, (CHUNK, CHUNK), 1)
        for g in range(A_GROUPS):
            wsp_s[g] = jnp.where(col <= row, wsp_ref[g], 0.0).astype(BF16)
        kprev_s[...] = jnp.zeros_like(kprev_s)
        vprev_s[...] = jnp.zeros_like(vprev_s)
        cxprev_s[...] = jnp.zeros_like(cxprev_s)

    x = x_ref[...]
    row = lambda ref: ref[layer:layer + 1, :]
    h = _rmsnorm(x, row(gpre_ref)).astype(BF16)

    def proj(off, width):
        return _dot(h, w_in_ref[:, off:off + width])

    va = _layernorm(jax.nn.gelu(proj(OFF_VA, D_MODEL)), row(lng_ref), row(lnb_ref)).astype(BF16)
    ua = jax.nn.gelu(proj(OFF_UA, D_MODEL))
    cx = proj(OFF_CG, D_MODEL) * proj(OFF_HB, D_MODEL)
    _cast_row_blocks(cast_src, cast_dst)
    prev = cxprev_s[...]
    rowi = jax.lax.broadcasted_iota(jnp.int32, cx.shape, 0)
    cx1 = jnp.where(rowi < 1, prev[SUBLANES - 1:SUBLANES, :], pltpu.roll(cx, 1, axis=0))
    cx2 = jnp.where(rowi < 2,
                    jnp.where(rowi < 1, prev[SUBLANES - 2:SUBLANES - 1, :],
                              prev[SUBLANES - 1:SUBLANES, :]),
                    pltpu.roll(cx, 2, axis=0))
    conv = row(convb_ref) + convw_ref[0:1, :] * cx2
    conv = conv + convw_ref[1:2, :] * cx1
    conv = conv + convw_ref[2:3, :] * cx
    cxprev_s[...] = cx[TM_MIX - SUBLANES:, :]
    cxlast_ref[...] = cx[TM_MIX - SUBLANES:, :]
    bx = (proj(OFF_BG, D_MODEL) * conv).astype(BF16)
    sa_rows = []
    for c in range(nblk):
        sa_rows.append(jnp.concatenate(
            [_dot(wsp_s[g], va[c * BLK:(c + 1) * BLK, g * LANES:(g + 1) * LANES])
             for g in range(A_GROUPS)], axis=1) + bsp_ref[...])
    ax = (ua * jnp.concatenate(sa_rows, axis=0)).astype(BF16)

    cos_a, sin_a = cos_a_ref[...], sin_a_ref[...]
    cos_t = cos_a * cos_b_ref[...] - sin_a * sin_b_ref[...]
    sin_s = sin_a * cos_bs_ref[...] + cos_a * sin_bs_ref[...]
    q = (_rope(proj(OFF_Q, N_HEADS * HEAD_DIM), cos_t, sin_s)
         * (LOG2_E * HEAD_DIM ** -0.5)).astype(BF16)
    k = _rope(proj(OFF_K, KV_DIM), cos_t, sin_s)
    v = proj(OFF_V, KV_DIM)
    klast_ref[...] = k[TM_MIX - BLK:, :]
    vlast_ref[...] = v[TM_MIX - BLK:, :]

    merged_ab, gate_c = [], []

    def gated_branch_pieces():
        for j in range(D_MODEL // GATE_COLS):
            cols = slice(j * GATE_COLS, (j + 1) * GATE_COLS)
            y_a = _dot(ax, wa_ref[:, cols])
            yield
            part = jax.nn.sigmoid(proj(OFF_GA + j * GATE_COLS, GATE_COLS)) * y_a
            yield
            y_b = _dot(bx, wb_ref[:, cols])
            yield
            merged_ab.append(part + jax.nn.sigmoid(proj(OFF_GB + j * GATE_COLS, GATE_COLS)) * y_b)
            yield
            gate_c.append(jax.nn.sigmoid(proj(OFF_GC + j * GATE_COLS, GATE_COLS)))
            yield

    pieces = gated_branch_pieces()
    n_pieces = 5 * (D_MODEL // GATE_COLS)
    for _ in range(PIECES_BEFORE_ATTN):
        next(pieces)

    qi = jax.lax.broadcasted_iota(jnp.int32, (BLK, 2 * BLK), 0)
    kj = jax.lax.broadcasted_iota(jnp.int32, (BLK, 2 * BLK), 1)
    band = (kj >= qi) & (kj <= qi + WINDOW)
    lane_low = _lane_iota((BLK, LANES)) < HEAD_DIM

    k_parts = [[(kprev_s[g, 0], kprev_s[g, 1]) for g in range(N_KV_HEADS)]]
    v_parts = [[(vprev_s[g, 0], vprev_s[g, 1]) for g in range(N_KV_HEADS)]]
    for c in range(nblk):
        k_parts.append(_split_kv_heads(k[c * BLK:(c + 1) * BLK, :]))
        v_parts.append(_split_kv_heads(v[c * BLK:(c + 1) * BLK, :]))
    for g in range(N_KV_HEADS):
        kprev_s[g, 0] = k_parts[nblk][g][0]
        kprev_s[g, 1] = k_parts[nblk][g][1]
        vprev_s[g, 0] = v_parts[nblk][g][0]
        vprev_s[g, 1] = v_parts[nblk][g][1]
    mask_first = band & (kj >= jnp.where(step == 0, BLK, 0))

    def block_diag(parts, c, g):
        return jnp.concatenate([parts[c][g][0], parts[c + 1][g][0],
                                parts[c][g][1], parts[c + 1][g][1]], axis=0)

    def scores(c, m):
        kbd = block_diag(k_parts, c, (2 * m) // Q_PER_KV)
        return _dot_nt(q[c * BLK:(c + 1) * BLK, m * LANES:(m + 1) * LANES], kbd)

    def softmax_numerators(c, m, s):
        mask = mask_first if c == 0 else band
        es, invs = [], []
        for hh in range(2):
            sink = sinks_ref[layer, 2 * m + hh] * LOG2_E
            sh = jnp.where(mask, s[:, hh * 2 * BLK:(hh + 1) * 2 * BLK], NEG_INF)
            mx = jnp.maximum(jnp.max(sh, axis=-1, keepdims=True), sink)
            e = jnp.exp2(sh - mx)
            den = jnp.sum(e, axis=-1, keepdims=True) + jnp.exp2(sink - mx)
            es.append(e.astype(BF16))
            invs.append(1.0 / den)
        return jnp.concatenate(es, axis=1), jnp.where(lane_low, invs[0], invs[1])

    def weighted_values(c, m, e, inv):
        vbd = block_diag(v_parts, c, (2 * m) // Q_PER_KV)
        return _dot(e, vbd) * inv

    its = [(c, m) for c in range(nblk) for m in range(N_HEADS // 2)]
    s_queue = [scores(*its[i]) for i in range(min(ATTN_LOOKAHEAD, len(its)))]
    o_parts = {}
    issued = PIECES_BEFORE_ATTN
    for i, (c, m) in enumerate(its):
        if i + ATTN_LOOKAHEAD < len(its):
            s_queue.append(scores(*its[i + ATTN_LOOKAHEAD]))
        e, inv = softmax_numerators(c, m, s_queue.pop(0))
        o_parts[(c, m)] = weighted_values(c, m, e, inv)
        while issued < PIECES_BEFORE_ATTN + ((i + 1) * (n_pieces - PIECES_BEFORE_ATTN)) // len(its):
            next(pieces)
            issued += 1
    o = jnp.concatenate(
        [jnp.concatenate([o_parts[(c, m)] for m in range(N_HEADS // 2)], axis=1)
         for c in range(nblk)], axis=0).astype(BF16)
    y_c = _dot(o, wc_ref[...])
    merged = jnp.concatenate(merged_ab, axis=1) + jnp.concatenate(gate_c, axis=1) * y_c

    merged = merged.astype(BF16)
    for c in range(nblk):
        rows = slice(c * BLK, (c + 1) * BLK)
        x1_ref[rows, :] = x[rows, :] + _rmsnorm(_dot(merged[rows, :], wo_ref[...]), row(gpost_ref))


def _prompt_mixer(layer, x, rope, sinks, gpre, gpost, w_in, lng, lnb, wsp, bsp, convw, convb,
                  wa, wb, wc, wo, cast_layer, cast_weights):
    n = x.shape[0]
    n_steps = n // TM_MIX
    lb = functools.partial(_layer_block, layer)
    casts = [_cast_specs(cast_layer, w, n_steps) for w in cast_weights]
    vec = _whole((DEPTH, D_MODEL))
    in_specs = [
        pl.BlockSpec((TM_MIX, D_MODEL), lambda i, *_: (i, 0)),
        pl.BlockSpec((None, 1, LANES), lambda i, *_: (i, 0, 0)),
        pl.BlockSpec((None, 1, LANES), lambda i, *_: (i, 0, 0)),
        _whole((TM_MIX, LANES)), _whole((TM_MIX, LANES)),
        _whole((TM_MIX, LANES)), _whole((TM_MIX, LANES)),
        vec, vec,
        _whole((D_MODEL, IN_WIDTH)),
        vec, vec,
        lb((A_GROUPS, CHUNK, CHUNK)), lb((CHUNK, D_MODEL)),
        lb((CONV_W, D_MODEL)), vec,
        _whole((D_MODEL, D_MODEL)), _whole((D_MODEL, D_MODEL)),
        _whole((D_MODEL, D_MODEL)), _whole((D_MODEL, D_MODEL)),
    ] + [c[0] for c in casts]
    assert len(in_specs) == N_MIXER_INPUTS + len(casts)
    out_specs = [
        pl.BlockSpec((TM_MIX, D_MODEL), lambda i, *_: (i, 0)),
        pl.BlockSpec((SUBLANES, D_MODEL), lambda i, *_: (0, 0)),
        pl.BlockSpec((BLK, KV_DIM), lambda i, *_: (0, 0)),
        pl.BlockSpec((BLK, KV_DIM), lambda i, *_: (0, 0)),
    ] + [c[1] for c in casts]
    out_shape = [
        jax.ShapeDtypeStruct((n, D_MODEL), F32),
        jax.ShapeDtypeStruct((SUBLANES, D_MODEL), F32),
        jax.ShapeDtypeStruct((BLK, KV_DIM), F32),
        jax.ShapeDtypeStruct((BLK, KV_DIM), F32),
    ] + [c[2] for c in casts]
    scratch = [
        pltpu.VMEM((A_GROUPS, CHUNK, CHUNK), BF16),
        pltpu.VMEM((N_KV_HEADS, 2, BLK, LANES), BF16),
        pltpu.VMEM((N_KV_HEADS, 2, BLK, LANES), BF16),
        pltpu.VMEM((SUBLANES, D_MODEL), F32),
    ]
    return pl.pallas_call(
        functools.partial(_mixer_kernel, layer, len(casts)),
        grid_spec=pltpu.PrefetchScalarGridSpec(
            num_scalar_prefetch=1, grid=(n_steps,), in_specs=in_specs, out_specs=out_specs,
            scratch_shapes=scratch),
        out_shape=out_shape,
        compiler_params=pltpu.CompilerParams(
            dimension_semantics=("arbitrary",), vmem_limit_bytes=V7X_VMEM_LIMIT),
        name="prompt_mixer",
    )(sinks, x, *rope, gpre, gpost, w_in, lng, lnb, wsp, bsp, convw, convb, wa, wb, wc, wo,
      *cast_weights)


def _swiglu(x, gpre, gpost, wg_ref, wu_ref, wd_ref):
    h = _rmsnorm(x, gpre).astype(BF16)
    a = jax.nn.silu(_dot(h, wg_ref[...])) * _dot(h, wu_ref[...])
    return x + _rmsnorm(_dot(a.astype(BF16), wd_ref[...]), gpost)


N_FFN_INPUTS = 13


def _ffn_kernel(layer, n_cast, *refs):
    (x_ref, gpre_ref, gpost_ref, wg_ref, wu_ref, wd_ref,
     xs_ref, os_ref, mab_ref, gc_ref, gpost_mix_ref, wc_ref, wo_ref) = refs[:N_FFN_INPUTS]
    cast_src = refs[N_FFN_INPUTS:N_FFN_INPUTS + n_cast]
    y_ref, ys_ref = refs[N_FFN_INPUTS + n_cast:N_FFN_INPUTS + n_cast + 2]
    cast_dst = refs[N_FFN_INPUTS + n_cast + 2:]
    row = lambda ref: ref[layer:layer + 1, :]
    rows = [slice(r * FFN_ROWS, (r + 1) * FFN_ROWS) for r in range(TM_FFN // FFN_ROWS)]
    xs = [x_ref[r, :] for r in rows]
    acts = []
    for x in xs:
        h = _rmsnorm(x, row(gpre_ref)).astype(BF16)
        acts.append((jax.nn.silu(_dot(h, wg_ref[...])) * _dot(h, wu_ref[...])).astype(BF16))
    for r, x, a in zip(rows, xs, acts):
        y_ref[r, :] = x + _rmsnorm(_dot(a, wd_ref[...]), row(gpost_ref))
    _cast_row_blocks(cast_src, cast_dst)

    @pl.when(pl.program_id(0) == pl.num_programs(0) - 1)
    def _sample_rows():
        y_c = _dot(os_ref[...].astype(BF16), wc_ref[...])
        merged = mab_ref[...] + gc_ref[...] * y_c
        x1 = xs_ref[...] + _rmsnorm(_dot(merged.astype(BF16), wo_ref[...]), row(gpost_mix_ref))
        ys_ref[...] = _swiglu(x1, row(gpre_ref), row(gpost_ref), wg_ref, wu_ref, wd_ref)


def _ffn(layer, x, gpre, gpost, wg, wu, wd, xs, o_s, mab, gc, gpost_mix, wc, wo,
         cast_layer, cast_weights):
    n = x.shape[0]
    n_steps = n // TM_FFN
    casts = [_cast_specs(cast_layer, w, n_steps) for w in cast_weights]
    row_tile = pl.BlockSpec((TM_FFN, D_MODEL), lambda i: (i, 0))
    sample_rows = _whole(xs.shape)
    vec = _whole((DEPTH, D_MODEL))
    return pl.pallas_call(
        functools.partial(_ffn_kernel, layer, len(casts)),
        grid=(n_steps,),
        in_specs=[row_tile, vec, vec,
                  _whole((D_MODEL, D_FF)), _whole((D_MODEL, D_FF)), _whole((D_FF, D_MODEL)),
                  sample_rows, sample_rows, sample_rows, sample_rows, vec,
                  _whole((D_MODEL, D_MODEL)), _whole((D_MODEL, D_MODEL))]
                 + [c[0] for c in casts],
        out_specs=[row_tile, pl.BlockSpec(xs.shape, lambda i: (0, 0))] + [c[1] for c in casts],
        out_shape=[jax.ShapeDtypeStruct((n, D_MODEL), F32), jax.ShapeDtypeStruct(xs.shape, F32)]
                  + [c[2] for c in casts],
        compiler_params=pltpu.CompilerParams(
            dimension_semantics=("arbitrary",), vmem_limit_bytes=V7X_VMEM_LIMIT),
        name="ffn",
    )(x, gpre, gpost, wg, wu, wd, xs, o_s, mab, gc, gpost_mix, wc, wo, *cast_weights)


def _sample_proj_kernel(layer, x_ref, hist_ref, cos_ref, sin_ref, gpre_ref, w_in_ref, lng_ref,
                        lnb_ref, wsp0_ref, bsp0_ref, convw_ref, convb_ref, wa_ref, wb_ref,
                        va_ref, newconv_ref, q_ref, k_ref, v_ref, mab_ref, gc_ref):
    row = lambda ref: ref[layer:layer + 1, :]
    x = x_ref[...]
    h = _rmsnorm(x, row(gpre_ref)).astype(BF16)

    def proj(off, width):
        return _dot(h, w_in_ref[:, off:off + width])

    ua = jax.nn.gelu(proj(OFF_UA, D_MODEL))
    va = _layernorm(jax.nn.gelu(proj(OFF_VA, D_MODEL)), row(lng_ref), row(lnb_ref))
    va_ref[...] = va
    sa = row(wsp0_ref) * va + row(bsp0_ref)
    y_a = _dot((ua * sa).astype(BF16), wa_ref[...])
    merged = jax.nn.sigmoid(proj(OFF_GA, D_MODEL)) * y_a

    cx = proj(OFF_CG, D_MODEL) * proj(OFF_HB, D_MODEL)
    h0 = hist_ref[:, 0:D_MODEL]
    h1 = hist_ref[:, D_MODEL:2 * D_MODEL]
    conv = row(convb_ref) + convw_ref[0:1, :] * h0
    conv = conv + convw_ref[1:2, :] * h1
    conv = conv + convw_ref[2:3, :] * cx
    newconv_ref[:, 0:D_MODEL] = h1
    newconv_ref[:, D_MODEL:2 * D_MODEL] = cx
    y_b = _dot((proj(OFF_BG, D_MODEL) * conv).astype(BF16), wb_ref[...])
    mab_ref[...] = merged + jax.nn.sigmoid(proj(OFF_GB, D_MODEL)) * y_b
    gc_ref[...] = jax.nn.sigmoid(proj(OFF_GC, D_MODEL))

    cos_t = cos_ref[...]
    sin_s = sin_ref[...]
    q_ref[...] = _rope(proj(OFF_Q, N_HEADS * HEAD_DIM), cos_t, sin_s) * (HEAD_DIM ** -0.5)
    k_ref[...] = _rope(proj(OFF_K, KV_DIM), cos_t, sin_s)
    v_ref[...] = proj(OFF_V, KV_DIM)


def _sample_proj(layer, x, hist, cos_t, sin_s, gpre, w_in, lng, lnb, wsp0, bsp0, convw, convb,
                 wa, wb):
    b = x.shape[0]
    lb = functools.partial(_layer_block, layer)
    vec = _whole((DEPTH, D_MODEL))
    in_specs = [
        _whole(x.shape), lb((b, (CONV_W - 1) * D_MODEL)), _whole(cos_t.shape), _whole(sin_s.shape),
        vec, _whole((D_MODEL, IN_WIDTH)), vec, vec,
        vec, vec, lb((CONV_W, D_MODEL)), vec,
        _whole((D_MODEL, D_MODEL)), _whole((D_MODEL, D_MODEL)),
    ]
    out_shape = [
        jax.ShapeDtypeStruct((b, D_MODEL), F32),
        jax.ShapeDtypeStruct((b, 2 * D_MODEL), F32),
        jax.ShapeDtypeStruct((b, D_MODEL), F32),
        jax.ShapeDtypeStruct((b, KV_DIM), F32),
        jax.ShapeDtypeStruct((b, KV_DIM), F32),
        jax.ShapeDtypeStruct((b, D_MODEL), F32),
        jax.ShapeDtypeStruct((b, D_MODEL), F32),
    ]
    return pl.pallas_call(
        functools.partial(_sample_proj_kernel, layer),
        grid=(1,),
        in_specs=in_specs,
        out_specs=[pl.BlockSpec(s.shape, lambda i: (0, 0)) for s in out_shape],
        out_shape=out_shape,
        compiler_params=pltpu.CompilerParams(
            dimension_semantics=("arbitrary",), vmem_limit_bytes=V7X_VMEM_LIMIT),
        name="sample_proj",
    )(x, hist, cos_t, sin_s, gpre, w_in, lng, lnb, wsp0, bsp0, convw, convb, wa, wb)


def _sample_attn_kernel(layer, n_aliased, sinks_ref, *refs):
    (q_ref, knew_ref, vnew_ref, knew_t_ref, vnew_t_ref, kc_ref, vc_ref,
     o_ref, kout_ref, vout_ref) = refs[n_aliased:]
    step = pl.program_id(0)
    qf = q_ref[...]
    s = jnp.einsum('grd,gdk->grk', qf.astype(BF16), kc_ref[...].astype(BF16),
                   preferred_element_type=F32)
    s_new = jnp.sum(qf * knew_ref[...], axis=-1, keepdims=True)
    gidx = jax.lax.broadcasted_iota(jnp.int32, (S2_GROUPS, Q_PER_KV, 1), 0)
    ridx = jax.lax.broadcasted_iota(jnp.int32, (S2_GROUPS, Q_PER_KV, 1), 1)
    head = (gidx % N_KV_HEADS) * Q_PER_KV + ridx
    sink = jnp.zeros((S2_GROUPS, Q_PER_KV, 1), F32)
    for hh in range(N_HEADS):
        sink = jnp.where(head == hh, sinks_ref[layer, hh], sink)
    mx = jnp.maximum(jnp.maximum(jnp.max(s, axis=-1, keepdims=True), s_new), sink)
    e = jnp.exp(s - mx)
    e_new = jnp.exp(s_new - mx)
    den = jnp.sum(e, axis=-1, keepdims=True) + e_new + jnp.exp(sink - mx)
    o = jnp.einsum('grk,gdk->grd', e.astype(BF16), vc_ref[...].astype(BF16),
                   preferred_element_type=F32)
    o_ref[...] = (o + e_new * vnew_ref[...]) * (1.0 / den)

    steps_per_tile = LANES // S2_GROUPS
    tile = pl.multiple_of((step // steps_per_tile) * LANES, LANES)
    first_col = (step % steps_per_tile) * S2_GROUPS
    last_lane = _lane_iota((HEAD_DIM, WINDOW)) == WINDOW - 1
    for new_t_ref, cache_ref, out_ref in ((knew_t_ref, kc_ref, kout_ref),
                                          (vnew_t_ref, vc_ref, vout_ref)):
        new_cols = new_t_ref[:, pl.ds(tile, LANES)]
        for j in range(S2_GROUPS):
            col = pltpu.roll(new_cols, WINDOW - 1 - first_col - j, axis=1)
            out_ref[j] = jnp.where(last_lane, col, pltpu.roll(cache_ref[j], WINDOW - 1, axis=1))


def _sample_attn(layer, sinks, q, knew, vnew, knew_t, vnew_t, kcache, vcache, kout_prev, vout_prev):
    groups = q.shape[0]
    group3 = lambda d1, d2: pl.BlockSpec((S2_GROUPS, d1, d2), lambda i, *_: (i, 0, 0))
    cache_blk = pl.BlockSpec((None, S2_GROUPS, HEAD_DIM, WINDOW), lambda i, *_: (layer, i, 0, 0))
    aliased = [] if kout_prev is None else [kout_prev, vout_prev]
    n_al = len(aliased)
    in_specs = ([pl.BlockSpec(memory_space=pl.ANY)] * n_al +
                [group3(Q_PER_KV, HEAD_DIM), group3(1, HEAD_DIM), group3(1, HEAD_DIM),
                 _whole(knew_t.shape), _whole(vnew_t.shape), cache_blk, cache_blk])
    io_alias = {1: 1, 2: 2} if n_al else {}
    return pl.pallas_call(
        functools.partial(_sample_attn_kernel, layer, n_al),
        grid_spec=pltpu.PrefetchScalarGridSpec(
            num_scalar_prefetch=1, grid=(groups // S2_GROUPS,),
            in_specs=in_specs,
            out_specs=[group3(Q_PER_KV, HEAD_DIM), cache_blk, cache_blk]),
        out_shape=[jax.ShapeDtypeStruct((groups, Q_PER_KV, HEAD_DIM), F32),
                   jax.ShapeDtypeStruct(kcache.shape, F32),
                   jax.ShapeDtypeStruct(vcache.shape, F32)],
        input_output_aliases=io_alias,
        compiler_params=pltpu.CompilerParams(
            dimension_semantics=("arbitrary",), vmem_limit_bytes=V7X_VMEM_LIMIT),
        name="sample_attn",
    )(sinks, *aliased, q, knew, vnew, knew_t, vnew_t, kcache, vcache)


def _rope_angles(pos):
    half = HEAD_DIM // 2
    inv = jnp.power(jnp.float32(ROPE_THETA), -jnp.arange(half, dtype=F32) * (2.0 / HEAD_DIM))
    return pos.astype(F32)[:, None] * jnp.tile(inv, LANES // half)[None, :]


def _rope_sign():
    lane = jnp.arange(LANES, dtype=jnp.int32)
    return jnp.where((lane % HEAD_DIM) < (HEAD_DIM // 2), -1.0, 1.0).astype(F32)[None, :]


def kernel(x_prompt, x_sample, state_conv, cache_win_k, cache_win_v, norm_pre_mix, norm_post_mix,
           norm_pre_ffn, norm_post_ffn, w_in, chunk_ln_g, chunk_ln_b, w_spatial, b_spatial, conv_w,
           conv_b, attn_sinks, w_br_a, w_br_b, w_br_c, w_out, w_ffn_gate, w_ffn_up, w_ffn_down):
    xp = x_prompt.reshape(SEQ, D_MODEL)
    xs = x_sample.reshape(DEC_BATCH, D_MODEL)

    sign = _rope_sign()
    ang_a = _rope_angles(jnp.arange(SEQ // TM_MIX, dtype=jnp.int32) * TM_MIX)
    ang_b = _rope_angles(jnp.arange(TM_MIX, dtype=jnp.int32))
    rope_p = (jnp.cos(ang_a)[:, None, :], jnp.sin(ang_a)[:, None, :],
              jnp.cos(ang_b), jnp.sin(ang_b), sign * jnp.cos(ang_b), sign * jnp.sin(ang_b))
    ang_s = _rope_angles(jnp.full((1,), PAST_LEN, dtype=jnp.int32))
    cos_s, sin_s = jnp.cos(ang_s), sign * jnp.sin(ang_s)

    mixer_f32 = (w_in, w_br_a, w_br_b, w_br_c, w_out)
    ffn_f32 = (w_ffn_gate, w_ffn_up, w_ffn_down)
    mixer_w = tuple(w[0].astype(BF16) for w in mixer_f32)
    gpre, gpost, gpre2, gpost2 = norm_pre_mix, norm_post_mix, norm_pre_ffn, norm_post_ffn
    lng, lnb, convb = chunk_ln_g, chunk_ln_b, conv_b
    bsp = jnp.repeat(jnp.transpose(b_spatial, (0, 2, 1)), A_GROUP_DIM, axis=2)
    wsp0 = jnp.repeat(w_spatial[:, :, 0, 0], A_GROUP_DIM, axis=1)
    bsp0 = jnp.repeat(b_spatial[:, :, 0], A_GROUP_DIM, axis=1)
    hist = state_conv.reshape(DEPTH, DEC_BATCH, (CONV_W - 1) * D_MODEL)
    n_groups = DEC_BATCH * N_KV_HEADS
    to_groups = lambda c: jnp.transpose(c, (0, 1, 3, 4, 2)).reshape(DEPTH, n_groups, HEAD_DIM, WINDOW)
    kcache, vcache = to_groups(cache_win_k), to_groups(cache_win_v)

    p_conv, p_k, p_v = [], [], []
    s_conv, s_cv = [], []
    knext = vnext = None
    for l in range(DEPTH):
        w_in_b, wa, wb, wc, wo = mixer_w
        x1, cxlast, klast, vlast, wg, wu, wd = _prompt_mixer(
            l, xp, rope_p, attn_sinks, gpre, gpost, w_in_b, lng, lnb, w_spatial, bsp,
            conv_w, convb, wa, wb, wc, wo, l, ffn_f32)
        p_conv.append(cxlast[SUBLANES - (CONV_W - 1):].reshape(1, CONV_W - 1, D_MODEL))
        p_k.append(klast.reshape(1, WINDOW, N_KV_HEADS, HEAD_DIM))
        p_v.append(vlast.reshape(1, WINDOW, N_KV_HEADS, HEAD_DIM))

        va, newconv, q, knew, vnew, mab, gc = _sample_proj(
            l, xs, hist, cos_s, sin_s, gpre, w_in_b, lng, lnb, wsp0, bsp0, conv_w, convb, wa, wb)
        knew = knew.reshape(n_groups, HEAD_DIM)
        vnew = vnew.reshape(n_groups, HEAD_DIM)
        o, knext, vnext = _sample_attn(
            l, attn_sinks, q.reshape(n_groups, Q_PER_KV, HEAD_DIM),
            knew[:, None, :], vnew[:, None, :], jnp.transpose(knew), jnp.transpose(vnew),
            kcache, vcache, knext, vnext)
        s_conv.append(newconv.reshape(DEC_BATCH, CONV_W - 1, D_MODEL))
        s_cv.append(va.reshape(DEC_BATCH, 1, D_MODEL))

        cast_next = (l + 1, mixer_f32) if l + 1 < DEPTH else (0, ())
        xp, xs, *mixer_w = _ffn(l, x1, gpre2, gpost2, wg, wu, wd,
                                xs, o.reshape(DEC_BATCH, D_MODEL), mab, gc, gpost, wc, wo,
                                *cast_next)

    from_groups = lambda c: jnp.transpose(
        c.reshape(DEPTH, DEC_BATCH, N_KV_HEADS, HEAD_DIM, WINDOW), (0, 1, 4, 2, 3))
    return (xp.reshape(1, SEQ, D_MODEL), xs.reshape(DEC_BATCH, 1, D_MODEL),
            jnp.stack(p_conv), jnp.stack(p_k), jnp.stack(p_v),
            jnp.stack(s_conv), from_groups(knext), from_groups(vnext), jnp.stack(s_cv))
```

```python
import functools

import jax
import jax.numpy as jnp
from jax.experimental import pallas as pl
from jax.experimental.pallas import tpu as pltpu

D_MODEL = 1024
SEQ = 16384
DEPTH = 2
DEC_BATCH = 128
PAST_LEN = 16384
CHUNK = 128
A_GROUPS = 8
A_GROUP_DIM = D_MODEL // A_GROUPS
CONV_W = 3
N_HEADS = 16
N_KV_HEADS = 4
HEAD_DIM = 64
Q_PER_KV = N_HEADS // N_KV_HEADS
WINDOW = 128
ROPE_THETA = 10000.0
D_FF = 2816
KV_DIM = N_KV_HEADS * HEAD_DIM
NEG_INF = -1e30
LOG2_E = 1.4426950408889634

OFF_UA = 0
OFF_VA = OFF_UA + D_MODEL
OFF_BG = OFF_VA + D_MODEL
OFF_CG = OFF_BG + D_MODEL
OFF_HB = OFF_CG + D_MODEL
OFF_Q = OFF_HB + D_MODEL
OFF_K = OFF_Q + N_HEADS * HEAD_DIM
OFF_V = OFF_K + KV_DIM
OFF_GA = OFF_V + KV_DIM
OFF_GB = OFF_GA + D_MODEL
OFF_GC = OFF_GB + D_MODEL
IN_WIDTH = OFF_GC + D_MODEL

LANES = 128
SUBLANES = 8
BF16_SUBLANES = 16
V7X_VMEM_LIMIT = 62 * 1024 * 1024

TM_MIX = 512
TM_FFN = 1024
FFN_ROWS = 256
BLK = WINDOW
S2_GROUPS = 128
ATTN_LOOKAHEAD = 2
GATE_COLS = 256
PIECES_BEFORE_ATTN = 3

assert CHUNK == BLK and WINDOW == BLK and PAST_LEN >= WINDOW
assert 2 * HEAD_DIM == LANES and A_GROUP_DIM == LANES
assert SEQ % TM_MIX == 0 and SEQ % TM_FFN == 0 and TM_MIX % BLK == 0
assert TM_FFN % FFN_ROWS == 0
assert LANES % S2_GROUPS == 0 and S2_GROUPS % N_KV_HEADS == 0 and WINDOW == LANES
assert (DEC_BATCH * N_KV_HEADS) % LANES == 0

BF16 = jnp.bfloat16
F32 = jnp.float32


def _dot(a, b):
    return jnp.dot(a, b, preferred_element_type=F32)


def _dot_nt(a, b):
    return jax.lax.dot_general(a, b, (((1,), (1,)), ((), ())), preferred_element_type=F32)


def _rmsnorm(x, g, eps=1e-6):
    return x * jax.lax.rsqrt(jnp.mean(x * x, axis=-1, keepdims=True) + eps) * g


def _layernorm(x, g, b, eps=1e-5):
    mu = jnp.mean(x, axis=-1, keepdims=True)
    xc = x - mu
    var = jnp.mean(xc * xc, axis=-1, keepdims=True)
    return xc * jax.lax.rsqrt(var + eps) * g + b


def _lane_iota(shape):
    return jax.lax.broadcasted_iota(jnp.int32, shape, len(shape) - 1)


def _rope_block(xb, cos_t, sin_s):
    lane = _lane_iota(xb.shape)
    first_half = (lane % HEAD_DIM) < (HEAD_DIM // 2)
    rot = jnp.where(first_half,
                    pltpu.roll(xb, LANES - HEAD_DIM // 2, axis=1),
                    pltpu.roll(xb, HEAD_DIM // 2, axis=1))
    return xb * cos_t + rot * sin_s


def _rope(x, cos_t, sin_s):
    n = x.shape[1] // LANES
    return jnp.concatenate(
        [_rope_block(x[:, j * LANES:(j + 1) * LANES], cos_t, sin_s) for j in range(n)], axis=1)


def _split_kv_heads(blk):
    out = []
    for j in range(KV_DIM // LANES):
        b = blk[:, j * LANES:(j + 1) * LANES]
        r = pltpu.roll(b, HEAD_DIM, axis=1)
        low = _lane_iota(b.shape) < HEAD_DIM
        zero = jnp.zeros_like(b)
        out.append((jnp.where(low, b, zero).astype(BF16), jnp.where(low, zero, r).astype(BF16)))
        out.append((jnp.where(low, r, zero).astype(BF16), jnp.where(low, zero, b).astype(BF16)))
    return out


def _layer_block(layer, shape):
    nd = len(shape)
    return pl.BlockSpec((None,) + tuple(shape), lambda *_: (layer,) + (0,) * nd,
                        pipeline_mode=pl.Buffered(1))


def _whole(shape):
    nd = len(shape)
    return pl.BlockSpec(tuple(shape), lambda *_: (0,) * nd, pipeline_mode=pl.Buffered(1))


N_MIXER_INPUTS = 20


def _cast_plan(rows, n_steps):
    for nb in range(n_steps, 0, -1):
        if rows % nb == 0 and (rows // nb) % BF16_SUBLANES == 0:
            return nb
    raise ValueError(f"no bf16-aligned row split of {rows} rows over {n_steps} steps")


def _cast_specs(layer, w_stacked, n_steps):
    _, rows, cols = w_stacked.shape
    nb = _cast_plan(rows, n_steps)
    blk = lambda i: jnp.minimum(i, nb - 1)
    return (pl.BlockSpec((None, rows // nb, cols), lambda i, *_: (layer, blk(i), 0)),
            pl.BlockSpec((rows // nb, cols), lambda i, *_: (blk(i), 0)),
            jax.ShapeDtypeStruct((rows, cols), BF16))


def _cast_row_blocks(src_refs, dst_refs):
    for src, dst in zip(src_refs, dst_refs):
        dst[...] = src[...].astype(BF16)


def _mixer_kernel(layer, n_cast, sinks_ref, *refs):
    (x_ref, cos_a_ref, sin_a_ref, cos_b_ref, sin_b_ref, cos_bs_ref, sin_bs_ref,
     gpre_ref, gpost_ref, w_in_ref, lng_ref, lnb_ref, wsp_ref, bsp_ref, convw_ref, convb_ref,
     wa_ref, wb_ref, wc_ref, wo_ref) = refs[:N_MIXER_INPUTS]
    cast_src = refs[N_MIXER_INPUTS:N_MIXER_INPUTS + n_cast]
    outs = refs[N_MIXER_INPUTS + n_cast:]
    x1_ref, cxlast_ref, klast_ref, vlast_ref = outs[:4]
    cast_dst = outs[4:4 + n_cast]
    wsp_s, kprev_s, vprev_s, cxprev_s = outs[4 + n_cast:]
    step = pl.program_id(0)
    nblk = TM_MIX // BLK

    @pl.when(step == 0)
    def _init():
        row = jax.lax.broadcasted_iota(jnp.int32, (CHUNK, CHUNK), 0)
        col = jax.lax.broadcasted_iota(jnp.int32, (CHUNK, CHUNK), 1)
        for g in range(A_GROUPS):
            wsp_s[g] = jnp.where(col <= row, wsp_ref[g], 0.0).astype(BF16)
        kprev_s[...] = jnp.zeros_like(kprev_s)
        vprev_s[...] = jnp.zeros_like(vprev_s)
        cxprev_s[...] = jnp.zeros_like(cxprev_s)

    x = x_ref[...]
    row = lambda ref: ref[layer:layer + 1, :]
    h = _rmsnorm(x, row(gpre_ref)).astype(BF16)

    def proj(off, width):
        return _dot(h, w_in_ref[:, off:off + width])

    va = _layernorm(jax.nn.gelu(proj(OFF_VA, D_MODEL)), row(lng_ref), row(lnb_ref)).astype(BF16)
    ua = jax.nn.gelu(proj(OFF_UA, D_MODEL))
    cx = proj(OFF_CG, D_MODEL) * proj(OFF_HB, D_MODEL)
    _cast_row_blocks(cast_src, cast_dst)
    prev = cxprev_s[...]
    rowi = jax.lax.broadcasted_iota(jnp.int32, cx.shape, 0)
    cx1 = jnp.where(rowi < 1, prev[SUBLANES - 1:SUBLANES, :], pltpu.roll(cx, 1, axis=0))
    cx2 = jnp.where(rowi < 2,
                    jnp.where(rowi < 1, prev[SUBLANES - 2:SUBLANES - 1, :],
                              prev[SUBLANES - 1:SUBLANES, :]),
                    pltpu.roll(cx, 2, axis=0))
    conv = row(convb_ref) + convw_ref[0:1, :] * cx2
    conv = conv + convw_ref[1:2, :] * cx1
    conv = conv + convw_ref[2:3, :] * cx
    cxprev_s[...] = cx[TM_MIX - SUBLANES:, :]
    cxlast_ref[...] = cx[TM_MIX - SUBLANES:, :]
    bx = (proj(OFF_BG, D_MODEL) * conv).astype(BF16)
    sa_rows = []
    for c in range(nblk):
        sa_rows.append(jnp.concatenate(
            [_dot(wsp_s[g], va[c * BLK:(c + 1) * BLK, g * LANES:(g + 1) * LANES])
             for g in range(A_GROUPS)], axis=1) + bsp_ref[...])
    ax = (ua * jnp.concatenate(sa_rows, axis=0)).astype(BF16)

    cos_a, sin_a = cos_a_ref[...], sin_a_ref[...]
    cos_t = cos_a * cos_b_ref[...] - sin_a * sin_b_ref[...]
    sin_s = sin_a * cos_bs_ref[...] + cos_a * sin_bs_ref[...]
    q = (_rope(proj(OFF_Q, N_HEADS * HEAD_DIM), cos_t, sin_s)
         * (LOG2_E * HEAD_DIM ** -0.5)).astype(BF16)
    k = _rope(proj(OFF_K, KV_DIM), cos_t, sin_s)
    v = proj(OFF_V, KV_DIM)
    klast_ref[...] = k[TM_MIX - BLK:, :]
    vlast_ref[...] = v[TM_MIX - BLK:, :]

    merged_ab, gate_c = [], []

    def gated_branch_pieces():
        for j in range(D_MODEL // GATE_COLS):
            cols = slice(j * GATE_COLS, (j + 1) * GATE_COLS)
            y_a = _dot(ax, wa_ref[:, cols])
            yield
            part = jax.nn.sigmoid(proj(OFF_GA + j * GATE_COLS, GATE_COLS)) * y_a
            yield
            y_b = _dot(bx, wb_ref[:, cols])
            yield
            merged_ab.append(part + jax.nn.sigmoid(proj(OFF_GB + j * GATE_COLS, GATE_COLS)) * y_b)
            yield
            gate_c.append(jax.nn.sigmoid(proj(OFF_GC + j * GATE_COLS, GATE_COLS)))
            yield

    pieces = gated_branch_pieces()
    n_pieces = 5 * (D_MODEL // GATE_COLS)
    for _ in range(PIECES_BEFORE_ATTN):
        next(pieces)

    qi = jax.lax.broadcasted_iota(jnp.int32, (BLK, 2 * BLK), 0)
    kj = jax.lax.broadcasted_iota(jnp.int32, (BLK, 2 * BLK), 1)
    band = (kj >= qi) & (kj <= qi + WINDOW)
    lane_low = _lane_iota((BLK, LANES)) < HEAD_DIM

    k_parts = [[(kprev_s[g, 0], kprev_s[g, 1]) for g in range(N_KV_HEADS)]]
    v_parts = [[(vprev_s[g, 0], vprev_s[g, 1]) for g in range(N_KV_HEADS)]]
    for c in range(nblk):
        k_parts.append(_split_kv_heads(k[c * BLK:(c + 1) * BLK, :]))
        v_parts.append(_split_kv_heads(v[c * BLK:(c + 1) * BLK, :]))
    for g in range(N_KV_HEADS):
        kprev_s[g, 0] = k_parts[nblk][g][0]
        kprev_s[g, 1] = k_parts[nblk][g][1]
        vprev_s[g, 0] = v_parts[nblk][g][0]
        vprev_s[g, 1] = v_parts[nblk][g][1]
    mask_first = band & (kj >= jnp.where(step == 0, BLK, 0))

    def block_diag(parts, c, g):
        return jnp.concatenate([parts[c][g][0], parts[c + 1][g][0],
                                parts[c][g][1], parts[c + 1][g][1]], axis=0)

    def scores(c, m):
        kbd = block_diag(k_parts, c, (2 * m) // Q_PER_KV)
        return _dot_nt(q[c * BLK:(c + 1) * BLK, m * LANES:(m + 1) * LANES], kbd)

    def softmax_numerators(c, m, s):
        mask = mask_first if c == 0 else band
        es, invs = [], []
        for hh in range(2):
            sink = sinks_ref[layer, 2 * m + hh] * LOG2_E
            sh = jnp.where(mask, s[:, hh * 2 * BLK:(hh + 1) * 2 * BLK], NEG_INF)
            mx = jnp.maximum(jnp.max(sh, axis=-1, keepdims=True), sink)
            e = jnp.exp2(sh - mx)
            den = jnp.sum(e, axis=-1, keepdims=True) + jnp.exp2(sink - mx)
            es.append(e.astype(BF16))
            invs.append(1.0 / den)
        return jnp.concatenate(es, axis=1), jnp.where(lane_low, invs[0], invs[1])

    def weighted_values(c, m, e, inv):
        vbd = block_diag(v_parts, c, (2 * m) // Q_PER_KV)
        return _dot(e, vbd) * inv

    its = [(c, m) for c in range(nblk) for m in range(N_HEADS // 2)]
    s_queue = [scores(*its[i]) for i in range(min(ATTN_LOOKAHEAD, len(its)))]
    o_parts = {}
    issued = PIECES_BEFORE_ATTN
    for i, (c, m) in enumerate(its):
        if i + ATTN_LOOKAHEAD < len(its):
            s_queue.append(scores(*its[i + ATTN_LOOKAHEAD]))
        e, inv = softmax_numerators(c, m, s_queue.pop(0))
        o_parts[(c, m)] = weighted_values(c, m, e, inv)
        while issued < PIECES_BEFORE_ATTN + ((i + 1) * (n_pieces - PIECES_BEFORE_ATTN)) // len(its):
            next(pieces)
            issued += 1
    o = jnp.concatenate(
        [jnp.concatenate([o_parts[(c, m)] for m in range(N_HEADS // 2)], axis=1)
         for c in range(nblk)], axis=0).astype(BF16)
    y_c = _dot(o, wc_ref[...])
    merged = jnp.concatenate(merged_ab, axis=1) + jnp.concatenate(gate_c, axis=1) * y_c

    merged = merged.astype(BF16)
    for c in range(nblk):
        rows = slice(c * BLK, (c + 1) * BLK)
        x1_ref[rows, :] = x[rows, :] + _rmsnorm(_dot(merged[rows, :], wo_ref[...]), row(gpost_ref))


def _prompt_mixer(layer, x, rope, sinks, gpre, gpost, w_in, lng, lnb, wsp, bsp, convw, convb,
                  wa, wb, wc, wo, cast_layer, cast_weights):
    n = x.shape[0]
    n_steps = n // TM_MIX
    lb = functools.partial(_layer_block, layer)
    casts = [_cast_specs(cast_layer, w, n_steps) for w in cast_weights]
    vec = _whole((DEPTH, D_MODEL))
    in_specs = [
        pl.BlockSpec((TM_MIX, D_MODEL), lambda i, *_: (i, 0)),
        pl.BlockSpec((None, 1, LANES), lambda i, *_: (i, 0, 0)),
        pl.BlockSpec((None, 1, LANES), lambda i, *_: (i, 0, 0)),
        _whole((TM_MIX, LANES)), _whole((TM_MIX, LANES)),
        _whole((TM_MIX, LANES)), _whole((TM_MIX, LANES)),
        vec, vec,
        _whole((D_MODEL, IN_WIDTH)),
        vec, vec,
        lb((A_GROUPS, CHUNK, CHUNK)), lb((CHUNK, D_MODEL)),
        lb((CONV_W, D_MODEL)), vec,
        _whole((D_MODEL, D_MODEL)), _whole((D_MODEL, D_MODEL)),
        _whole((D_MODEL, D_MODEL)), _whole((D_MODEL, D_MODEL)),
    ] + [c[0] for c in casts]
    assert len(in_specs) == N_MIXER_INPUTS + len(casts)
    out_specs = [
        pl.BlockSpec((TM_MIX, D_MODEL), lambda i, *_: (i, 0)),
        pl.BlockSpec((SUBLANES, D_MODEL), lambda i, *_: (0, 0)),
        pl.BlockSpec((BLK, KV_DIM), lambda i, *_: (0, 0)),
        pl.BlockSpec((BLK, KV_DIM), lambda i, *_: (0, 0)),
    ] + [c[1] for c in casts]
    out_shape = [
        jax.ShapeDtypeStruct((n, D_MODEL), F32),
        jax.ShapeDtypeStruct((SUBLANES, D_MODEL), F32),
        jax.ShapeDtypeStruct((BLK, KV_DIM), F32),
        jax.ShapeDtypeStruct((BLK, KV_DIM), F32),
    ] + [c[2] for c in casts]
    scratch = [
        pltpu.VMEM((A_GROUPS, CHUNK, CHUNK), BF16),
        pltpu.VMEM((N_KV_HEADS, 2, BLK, LANES), BF16),
        pltpu.VMEM((N_KV_HEADS, 2, BLK, LANES), BF16),
        pltpu.VMEM((SUBLANES, D_MODEL), F32),
    ]
    return pl.pallas_call(
        functools.partial(_mixer_kernel, layer, len(casts)),
        grid_spec=pltpu.PrefetchScalarGridSpec(
            num_scalar_prefetch=1, grid=(n_steps,), in_specs=in_specs, out_specs=out_specs,
            scratch_shapes=scratch),
        out_shape=out_shape,
        compiler_params=pltpu.CompilerParams(
            dimension_semantics=("arbitrary",), vmem_limit_bytes=V7X_VMEM_LIMIT),
        name="prompt_mixer",
    )(sinks, x, *rope, gpre, gpost, w_in, lng, lnb, wsp, bsp, convw, convb, wa, wb, wc, wo,
      *cast_weights)


def _swiglu(x, gpre, gpost, wg_ref, wu_ref, wd_ref):
    h = _rmsnorm(x, gpre).astype(BF16)
    a = jax.nn.silu(_dot(h, wg_ref[...])) * _dot(h, wu_ref[...])
    return x + _rmsnorm(_dot(a.astype(BF16), wd_ref[...]), gpost)


N_FFN_INPUTS = 13


def _ffn_kernel(layer, n_cast, *refs):
    (x_ref, gpre_ref, gpost_ref, wg_ref, wu_ref, wd_ref,
     xs_ref, os_ref, mab_ref, gc_ref, gpost_mix_ref, wc_ref, wo_ref) = refs[:N_FFN_INPUTS]
    cast_src = refs[N_FFN_INPUTS:N_FFN_INPUTS + n_cast]
    y_ref, ys_ref = refs[N_FFN_INPUTS + n_cast:N_FFN_INPUTS + n_cast + 2]
    cast_dst = refs[N_FFN_INPUTS + n_cast + 2:]
    row = lambda ref: ref[layer:layer + 1, :]
    rows = [slice(r * FFN_ROWS, (r + 1) * FFN_ROWS) for r in range(TM_FFN // FFN_ROWS)]
    xs = [x_ref[r, :] for r in rows]
    acts = []
    for x in xs:
        h = _rmsnorm(x, row(gpre_ref)).astype(BF16)
        acts.append((jax.nn.silu(_dot(h, wg_ref[...])) * _dot(h, wu_ref[...])).astype(BF16))
    for r, x, a in zip(rows, xs, acts):
        y_ref[r, :] = x + _rmsnorm(_dot(a, wd_ref[...]), row(gpost_ref))
    _cast_row_blocks(cast_src, cast_dst)

    @pl.when(pl.program_id(0) == pl.num_programs(0) - 1)
    def _sample_rows():
        y_c = _dot(os_ref[...].astype(BF16), wc_ref[...])
        merged = mab_ref[...] + gc_ref[...] * y_c
        x1 = xs_ref[...] + _rmsnorm(_dot(merged.astype(BF16), wo_ref[...]), row(gpost_mix_ref))
        ys_ref[...] = _swiglu(x1, row(gpre_ref), row(gpost_ref), wg_ref, wu_ref, wd_ref)


def _ffn(layer, x, gpre, gpost, wg, wu, wd, xs, o_s, mab, gc, gpost_mix, wc, wo,
         cast_layer, cast_weights):
    n = x.shape[0]
    n_steps = n // TM_FFN
    casts = [_cast_specs(cast_layer, w, n_steps) for w in cast_weights]
    row_tile = pl.BlockSpec((TM_FFN, D_MODEL), lambda i: (i, 0))
    sample_rows = _whole(xs.shape)
    vec = _whole((DEPTH, D_MODEL))
    return pl.pallas_call(
        functools.partial(_ffn_kernel, layer, len(casts)),
        grid=(n_steps,),
        in_specs=[row_tile, vec, vec,
                  _whole((D_MODEL, D_FF)), _whole((D_MODEL, D_FF)), _whole((D_FF, D_MODEL)),
                  sample_rows, sample_rows, sample_rows, sample_rows, vec,
                  _whole((D_MODEL, D_MODEL)), _whole((D_MODEL, D_MODEL))]
                 + [c[0] for c in casts],
        out_specs=[row_tile, pl.BlockSpec(xs.shape, lambda i: (0, 0))] + [c[1] for c in casts],
        out_shape=[jax.ShapeDtypeStruct((n, D_MODEL), F32), jax.ShapeDtypeStruct(xs.shape, F32)]
                  + [c[2] for c in casts],
        compiler_params=pltpu.CompilerParams(
            dimension_semantics=("arbitrary",), vmem_limit_bytes=V7X_VMEM_LIMIT),
        name="ffn",
    )(x, gpre, gpost, wg, wu, wd, xs, o_s, mab, gc, gpost_mix, wc, wo, *cast_weights)


def _sample_proj_kernel(layer, x_ref, hist_ref, cos_ref, sin_ref, gpre_ref, w_in_ref, lng_ref,
                        lnb_ref, wsp0_ref, bsp0_ref, convw_ref, convb_ref, wa_ref, wb_ref,
                        va_ref, newconv_ref, q_ref, k_ref, v_ref, mab_ref, gc_ref):
    row = lambda ref: ref[layer:layer + 1, :]
    x = x_ref[...]
    h = _rmsnorm(x, row(gpre_ref)).astype(BF16)

    def proj(off, width):
        return _dot(h, w_in_ref[:, off:off + width])

    ua = jax.nn.gelu(proj(OFF_UA, D_MODEL))
    va = _layernorm(jax.nn.gelu(proj(OFF_VA, D_MODEL)), row(lng_ref), row(lnb_ref))
    va_ref[...] = va
    sa = row(wsp0_ref) * va + row(bsp0_ref)
    y_a = _dot((ua * sa).astype(BF16), wa_ref[...])
    merged = jax.nn.sigmoid(proj(OFF_GA, D_MODEL)) * y_a

    cx = proj(OFF_CG, D_MODEL) * proj(OFF_HB, D_MODEL)
    h0 = hist_ref[:, 0:D_MODEL]
    h1 = hist_ref[:, D_MODEL:2 * D_MODEL]
    conv = row(convb_ref) + convw_ref[0:1, :] * h0
    conv = conv + convw_ref[1:2, :] * h1
    conv = conv + convw_ref[2:3, :] * cx
    newconv_ref[:, 0:D_MODEL] = h1
    newconv_ref[:, D_MODEL:2 * D_MODEL] = cx
    y_b = _dot((proj(OFF_BG, D_MODEL) * conv).astype(BF16), wb_ref[...])
    mab_ref[...] = merged + jax.nn.sigmoid(proj(OFF_GB, D_MODEL)) * y_b
    gc_ref[...] = jax.nn.sigmoid(proj(OFF_GC, D_MODEL))

    cos_t = cos_ref[...]
    sin_s = sin_ref[...]
    q_ref[...] = _rope(proj(OFF_Q, N_HEADS * HEAD_DIM), cos_t, sin_s) * (HEAD_DIM ** -0.5)
    k_ref[...] = _rope(proj(OFF_K, KV_DIM), cos_t, sin_s)
    v_ref[...] = proj(OFF_V, KV_DIM)


def _sample_proj(layer, x, hist, cos_t, sin_s, gpre, w_in, lng, lnb, wsp0, bsp0, convw, convb,
                 wa, wb):
    b = x.shape[0]
    lb = functools.partial(_layer_block, layer)
    vec = _whole((DEPTH, D_MODEL))
    in_specs = [
        _whole(x.shape), lb((b, (CONV_W - 1) * D_MODEL)), _whole(cos_t.shape), _whole(sin_s.shape),
        vec, _whole((D_MODEL, IN_WIDTH)), vec, vec,
        vec, vec, lb((CONV_W, D_MODEL)), vec,
        _whole((D_MODEL, D_MODEL)), _whole((D_MODEL, D_MODEL)),
    ]
    out_shape = [
        jax.ShapeDtypeStruct((b, D_MODEL), F32),
        jax.ShapeDtypeStruct((b, 2 * D_MODEL), F32),
        jax.ShapeDtypeStruct((b, D_MODEL), F32),
        jax.ShapeDtypeStruct((b, KV_DIM), F32),
        jax.ShapeDtypeStruct((b, KV_DIM), F32),
        jax.ShapeDtypeStruct((b, D_MODEL), F32),
        jax.ShapeDtypeStruct((b, D_MODEL), F32),
    ]
    return pl.pallas_call(
        functools.partial(_sample_proj_kernel, layer),
        grid=(1,),
        in_specs=in_specs,
        out_specs=[pl.BlockSpec(s.shape, lambda i: (0, 0)) for s in out_shape],
        out_shape=out_shape,
        compiler_params=pltpu.CompilerParams(
            dimension_semantics=("arbitrary",), vmem_limit_bytes=V7X_VMEM_LIMIT),
        name="sample_proj",
    )(x, hist, cos_t, sin_s, gpre, w_in, lng, lnb, wsp0, bsp0, convw, convb, wa, wb)


def _sample_attn_kernel(layer, n_aliased, sinks_ref, *refs):
    (q_ref, knew_ref, vnew_ref, knew_t_ref, vnew_t_ref, kc_ref, vc_ref,
     o_ref, kout_ref, vout_ref) = refs[n_aliased:]
    step = pl.program_id(0)
    qf = q_ref[...]
    s = jnp.einsum('grd,gdk->grk', qf.astype(BF16), kc_ref[...].astype(BF16),
                   preferred_element_type=F32)
    s_new = jnp.sum(qf * knew_ref[...], axis=-1, keepdims=True)
    gidx = jax.lax.broadcasted_iota(jnp.int32, (S2_GROUPS, Q_PER_KV, 1), 0)
    ridx = jax.lax.broadcasted_iota(jnp.int32, (S2_GROUPS, Q_PER_KV, 1), 1)
    head = (gidx % N_KV_HEADS) * Q_PER_KV + ridx
    sink = jnp.zeros((S2_GROUPS, Q_PER_KV, 1), F32)
    for hh in range(N_HEADS):
        sink = jnp.where(head == hh, sinks_ref[layer, hh], sink)
    mx = jnp.maximum(jnp.maximum(jnp.max(s, axis=-1, keepdims=True), s_new), sink)
    e = jnp.exp(s - mx)
    e_new = jnp.exp(s_new - mx)
    den = jnp.sum(e, axis=-1, keepdims=True) + e_new + jnp.exp(sink - mx)
    o = jnp.einsum('grk,gdk->grd', e.astype(BF16), vc_ref[...].astype(BF16),
                   preferred_element_type=F32)
    o_ref[...] = (o + e_new * vnew_ref[...]) * (1.0 / den)

    steps_per_tile = LANES // S2_GROUPS
    tile = pl.multiple_of((step // steps_per_tile) * LANES, LANES)
    first_col = (step % steps_per_tile) * S2_GROUPS
    last_lane = _lane_iota((HEAD_DIM, WINDOW)) == WINDOW - 1
    for new_t_ref, cache_ref, out_ref in ((knew_t_ref, kc_ref, kout_ref),
                                          (vnew_t_ref, vc_ref, vout_ref)):
        new_cols = new_t_ref[:, pl.ds(tile, LANES)]
        for j in range(S2_GROUPS):
            col = pltpu.roll(new_cols, WINDOW - 1 - first_col - j, axis=1)
            out_ref[j] = jnp.where(last_lane, col, pltpu.roll(cache_ref[j], WINDOW - 1, axis=1))


def _sample_attn(layer, sinks, q, knew, vnew, knew_t, vnew_t, kcache, vcache, kout_prev, vout_prev):
    groups = q.shape[0]
    group3 = lambda d1, d2: pl.BlockSpec((S2_GROUPS, d1, d2), lambda i, *_: (i, 0, 0))
    cache_blk = pl.BlockSpec((None, S2_GROUPS, HEAD_DIM, WINDOW), lambda i, *_: (layer, i, 0, 0))
    aliased = [] if kout_prev is None else [kout_prev, vout_prev]
    n_al = len(aliased)
    in_specs = ([pl.BlockSpec(memory_space=pl.ANY)] * n_al +
                [group3(Q_PER_KV, HEAD_DIM), group3(1, HEAD_DIM), group3(1, HEAD_DIM),
                 _whole(knew_t.shape), _whole(vnew_t.shape), cache_blk, cache_blk])
    io_alias = {1: 1, 2: 2} if n_al else {}
    return pl.pallas_call(
        functools.partial(_sample_attn_kernel, layer, n_al),
        grid_spec=pltpu.PrefetchScalarGridSpec(
            num_scalar_prefetch=1, grid=(groups // S2_GROUPS,),
            in_specs=in_specs,
            out_specs=[group3(Q_PER_KV, HEAD_DIM), cache_blk, cache_blk]),
        out_shape=[jax.ShapeDtypeStruct((groups, Q_PER_KV, HEAD_DIM), F32),
                   jax.ShapeDtypeStruct(kcache.shape, F32),
                   jax.ShapeDtypeStruct(vcache.shape, F32)],
        input_output_aliases=io_alias,
        compiler_params=pltpu.CompilerParams(
            dimension_semantics=("arbitrary",), vmem_limit_bytes=V7X_VMEM_LIMIT),
        name="sample_attn",
    )(sinks, *aliased, q, knew, vnew, knew_t, vnew_t, kcache, vcache)


def _rope_angles(pos):
    half = HEAD_DIM // 2
    inv = jnp.power(jnp.float32(ROPE_THETA), -jnp.arange(half, dtype=F32) * (2.0 / HEAD_DIM))
    return pos.astype(F32)[:, None] * jnp.tile(inv, LANES // half)[None, :]


def _rope_sign():
    lane = jnp.arange(LANES, dtype=jnp.int32)
    return jnp.where((lane % HEAD_DIM) < (HEAD_DIM // 2), -1.0, 1.0).astype(F32)[None, :]


def kernel(x_prompt, x_sample, state_conv, cache_win_k, cache_win_v, norm_pre_mix, norm_post_mix,
           norm_pre_ffn, norm_post_ffn, w_in, chunk_ln_g, chunk_ln_b, w_spatial, b_spatial, conv_w,
           conv_b, attn_sinks, w_br_a, w_br_b, w_br_c, w_out, w_ffn_gate, w_ffn_up, w_ffn_down):
    xp = x_prompt.reshape(SEQ, D_MODEL)
    xs = x_sample.reshape(DEC_BATCH, D_MODEL)

    sign = _rope_sign()
    ang_a = _rope_angles(jnp.arange(SEQ // TM_MIX, dtype=jnp.int32) * TM_MIX)
    ang_b = _rope_angles(jnp.arange(TM_MIX, dtype=jnp.int32))
    rope_p = (jnp.cos(ang_a)[:, None, :], jnp.sin(ang_a)[:, None, :],
              jnp.cos(ang_b), jnp.sin(ang_b), sign * jnp.cos(ang_b), sign * jnp.sin(ang_b))
    ang_s = _rope_angles(jnp.full((1,), PAST_LEN, dtype=jnp.int32))
    cos_s, sin_s = jnp.cos(ang_s), sign * jnp.sin(ang_s)

    mixer_f32 = (w_in, w_br_a, w_br_b, w_br_c, w_out)
    ffn_f32 = (w_ffn_gate, w_ffn_up, w_ffn_down)
    mixer_w = tuple(w[0].astype(BF16) for w in mixer_f32)
    gpre, gpost, gpre2, gpost2 = norm_pre_mix, norm_post_mix, norm_pre_ffn, norm_post_ffn
    lng, lnb, convb = chunk_ln_g, chunk_ln_b, conv_b
    bsp = jnp.repeat(jnp.transpose(b_spatial, (0, 2, 1)), A_GROUP_DIM, axis=2)
    wsp0 = jnp.repeat(w_spatial[:, :, 0, 0], A_GROUP_DIM, axis=1)
    bsp0 = jnp.repeat(b_spatial[:, :, 0], A_GROUP_DIM, axis=1)
    hist = state_conv.reshape(DEPTH, DEC_BATCH, (CONV_W - 1) * D_MODEL)
    n_groups = DEC_BATCH * N_KV_HEADS
    to_groups = lambda c: jnp.transpose(c, (0, 1, 3, 4, 2)).reshape(DEPTH, n_groups, HEAD_DIM, WINDOW)
    kcache, vcache = to_groups(cache_win_k), to_groups(cache_win_v)

    p_conv, p_k, p_v = [], [], []
    s_conv, s_cv = [], []
    knext = vnext = None
    for l in range(DEPTH):
        w_in_b, wa, wb, wc, wo = mixer_w
        x1, cxlast, klast, vlast, wg, wu, wd = _prompt_mixer(
            l, xp, rope_p, attn_sinks, gpre, gpost, w_in_b, lng, lnb, w_spatial, bsp,
            conv_w, convb, wa, wb, wc, wo, l, ffn_f32)
        p_conv.append(cxlast[SUBLANES - (CONV_W - 1):].reshape(1, CONV_W - 1, D_MODEL))
        p_k.append(klast.reshape(1, WINDOW, N_KV_HEADS, HEAD_DIM))
        p_v.append(vlast.reshape(1, WINDOW, N_KV_HEADS, HEAD_DIM))

        va, newconv, q, knew, vnew, mab, gc = _sample_proj(
            l, xs, hist, cos_s, sin_s, gpre, w_in_b, lng, lnb, wsp0, bsp0, conv_w, convb, wa, wb)
        knew = knew.reshape(n_groups, HEAD_DIM)
        vnew = vnew.reshape(n_groups, HEAD_DIM)
        o, knext, vnext = _sample_attn(
            l, attn_sinks, q.reshape(n_groups, Q_PER_KV, HEAD_DIM),
            knew[:, None, :], vnew[:, None, :], jnp.transpose(knew), jnp.transpose(vnew),
            kcache, vcache, knext, vnext)
        s_conv.append(newconv.reshape(DEC_BATCH, CONV_W - 1, D_MODEL))
        s_cv.append(va.reshape(DEC_BATCH, 1, D_MODEL))

        cast_next = (l + 1, mixer_f32) if l + 1 < DEPTH else (0, ())
        xp, xs, *mixer_w = _ffn(l, x1, gpre2, gpost2, wg, wu, wd,
                                xs, o.reshape(DEC_BATCH, D_MODEL), mab, gc, gpost, wc, wo,
                                *cast_next)

    from_groups = lambda c: jnp.transpose(
        c.reshape(DEPTH, DEC_BATCH, N_KV_HEADS, HEAD_DIM, WINDOW), (0, 1, 4, 2, 3))
    return (xp.reshape(1, SEQ, D_MODEL), xs.reshape(DEC_BATCH, 1, D_MODEL),
            jnp.stack(p_conv), jnp.stack(p_k), jnp.stack(p_v),
            jnp.stack(s_conv), from_groups(knext), from_groups(vnext), jnp.stack(s_cv))
```

```python
import functools

import jax
import jax.numpy as jnp
from jax.experimental import pallas as pl
from jax.experimental.pallas import tpu as pltpu

D_MODEL = 1024
SEQ = 16384
DEPTH = 2
DEC_BATCH = 128
PAST_LEN = 16384
CHUNK = 128
A_GROUPS = 8
A_GROUP_DIM = D_MODEL // A_GROUPS
CONV_W = 3
N_HEADS = 16
N_KV_HEADS = 4
HEAD_DIM = 64
Q_PER_KV = N_HEADS // N_KV_HEADS
WINDOW = 128
ROPE_THETA = 10000.0
D_FF = 2816
KV_DIM = N_KV_HEADS * HEAD_DIM
NEG_INF = -1e30
LOG2_E = 1.4426950408889634

OFF_UA = 0
OFF_VA = OFF_UA + D_MODEL
OFF_BG = OFF_VA + D_MODEL
OFF_CG = OFF_BG + D_MODEL
OFF_HB = OFF_CG + D_MODEL
OFF_Q = OFF_HB + D_MODEL
OFF_K = OFF_Q + N_HEADS * HEAD_DIM
OFF_V = OFF_K + KV_DIM
OFF_GA = OFF_V + KV_DIM
OFF_GB = OFF_GA + D_MODEL
OFF_GC = OFF_GB + D_MODEL
IN_WIDTH = OFF_GC + D_MODEL

LANES = 128
SUBLANES = 8
BF16_SUBLANES = 16
V7X_VMEM_LIMIT = 62 * 1024 * 1024

TM_MIX = 512
TM_FFN = 1024
FFN_ROWS = 256
BLK = WINDOW
S2_GROUPS = 64
ATTN_LOOKAHEAD = 2
GATE_COLS = 256
PIECES_BEFORE_ATTN = 3

assert CHUNK == BLK and WINDOW == BLK and PAST_LEN >= WINDOW
assert 2 * HEAD_DIM == LANES and A_GROUP_DIM == LANES
assert SEQ % TM_MIX == 0 and SEQ % TM_FFN == 0 and TM_MIX % BLK == 0
assert TM_FFN % FFN_ROWS == 0
assert LANES % S2_GROUPS == 0 and S2_GROUPS % N_KV_HEADS == 0 and WINDOW == LANES
assert (DEC_BATCH * N_KV_HEADS) % LANES == 0

BF16 = jnp.bfloat16
F32 = jnp.float32


def _dot(a, b):
    return jnp.dot(a, b, preferred_element_type=F32)


def _dot_nt(a, b):
    return jax.lax.dot_general(a, b, (((1,), (1,)), ((), ())), preferred_element_type=F32)


def _rmsnorm(x, g, eps=1e-6):
    return x * jax.lax.rsqrt(jnp.mean(x * x, axis=-1, keepdims=True) + eps) * g


def _layernorm(x, g, b, eps=1e-5):
    mu = jnp.mean(x, axis=-1, keepdims=True)
    xc = x - mu
    var = jnp.mean(xc * xc, axis=-1, keepdims=True)
    return xc * jax.lax.rsqrt(var + eps) * g + b


def _lane_iota(shape):
    return jax.lax.broadcasted_iota(jnp.int32, shape, len(shape) - 1)


def _rope_block(xb, cos_t, sin_s):
    lane = _lane_iota(xb.shape)
    first_half = (lane % HEAD_DIM) < (HEAD_DIM // 2)
    rot = jnp.where(first_half,
                    pltpu.roll(xb, LANES - HEAD_DIM // 2, axis=1),
                    pltpu.roll(xb, HEAD_DIM // 2, axis=1))
    return xb * cos_t + rot * sin_s


def _rope(x, cos_t, sin_s):
    n = x.shape[1] // LANES
    return jnp.concatenate(
        [_rope_block(x[:, j * LANES:(j + 1) * LANES], cos_t, sin_s) for j in range(n)], axis=1)


def _split_kv_heads(blk):
    out = []
    for j in range(KV_DIM // LANES):
        b = blk[:, j * LANES:(j + 1) * LANES]
        r = pltpu.roll(b, HEAD_DIM, axis=1)
        low = _lane_iota(b.shape) < HEAD_DIM
        zero = jnp.zeros_like(b)
        out.append((jnp.where(low, b, zero).astype(BF16), jnp.where(low, zero, r).astype(BF16)))
        out.append((jnp.where(low, r, zero).astype(BF16), jnp.where(low, zero, b).astype(BF16)))
    return out


def _layer_block(layer, shape):
    nd = len(shape)
    return pl.BlockSpec((None,) + tuple(shape), lambda *_: (layer,) + (0,) * nd,
                        pipeline_mode=pl.Buffered(1))


def _whole(shape):
    nd = len(shape)
    return pl.BlockSpec(tuple(shape), lambda *_: (0,) * nd, pipeline_mode=pl.Buffered(1))


N_MIXER_INPUTS = 20


def _cast_plan(rows, n_steps):
    for nb in range(n_steps, 0, -1):
        if rows % nb == 0 and (rows // nb) % BF16_SUBLANES == 0:
            return nb
    raise ValueError(f"no bf16-aligned row split of {rows} rows over {n_steps} steps")


def _cast_specs(layer, w_stacked, n_steps):
    _, rows, cols = w_stacked.shape
    nb = _cast_plan(rows, n_steps)
    blk = lambda i: jnp.minimum(i, nb - 1)
    return (pl.BlockSpec((None, rows // nb, cols), lambda i, *_: (layer, blk(i), 0)),
            pl.BlockSpec((rows // nb, cols), lambda i, *_: (blk(i), 0)),
            jax.ShapeDtypeStruct((rows, cols), BF16))


def _cast_row_blocks(src_refs, dst_refs):
    for src, dst in zip(src_refs, dst_refs):
        dst[...] = src[...].astype(BF16)


def _mixer_kernel(layer, n_cast, sinks_ref, *refs):
    (x_ref, cos_a_ref, sin_a_ref, cos_b_ref, sin_b_ref, cos_bs_ref, sin_bs_ref,
     gpre_ref, gpost_ref, w_in_ref, lng_ref, lnb_ref, wsp_ref, bsp_ref, convw_ref, convb_ref,
     wa_ref, wb_ref, wc_ref, wo_ref) = refs[:N_MIXER_INPUTS]
    cast_src = refs[N_MIXER_INPUTS:N_MIXER_INPUTS + n_cast]
    outs = refs[N_MIXER_INPUTS + n_cast:]
    x1_ref, cxlast_ref, klast_ref, vlast_ref = outs[:4]
    cast_dst = outs[4:4 + n_cast]
    wsp_s, kprev_s, vprev_s, cxprev_s = outs[4 + n_cast:]
    step = pl.program_id(0)
    nblk = TM_MIX // BLK

    @pl.when(step == 0)
    def _init():
        row = jax.lax.broadcasted_iota(jnp.int32, (CHUNK, CHUNK), 0)
        col = jax.lax.broadcasted_iota(jnp.int32, (CHUNK, CHUNK), 1)
        for g in range(A_GROUPS):
            wsp_s[g] = jnp.where(col <= row, wsp_ref[g], 0.0).astype(BF16)
        kprev_s[...] = jnp.zeros_like(kprev_s)
        vprev_s[...] = jnp.zeros_like(vprev_s)
        cxprev_s[...] = jnp.zeros_like(cxprev_s)

    x = x_ref[...]
    row = lambda ref: ref[layer:layer + 1, :]
    h = _rmsnorm(x, row(gpre_ref)).astype(BF16)

    def proj(off, width):
        return _dot(h, w_in_ref[:, off:off + width])

    va = _layernorm(jax.nn.gelu(proj(OFF_VA, D_MODEL)), row(lng_ref), row(lnb_ref)).astype(BF16)
    ua = jax.nn.gelu(proj(OFF_UA, D_MODEL))
    cx = proj(OFF_CG, D_MODEL) * proj(OFF_HB, D_MODEL)
    _cast_row_blocks(cast_src, cast_dst)
    prev = cxprev_s[...]
    rowi = jax.lax.broadcasted_iota(jnp.int32, cx.shape, 0)
    cx1 = jnp.where(rowi < 1, prev[SUBLANES - 1:SUBLANES, :], pltpu.roll(cx, 1, axis=0))
    cx2 = jnp.where(rowi < 2,
                    jnp.where(rowi < 1, prev[SUBLANES - 2:SUBLANES - 1, :],
                              prev[SUBLANES - 1:SUBLANES, :]),
                    pltpu.roll(cx, 2, axis=0))
    conv = row(convb_ref) + convw_ref[0:1, :] * cx2
    conv = conv + convw_ref[1:2, :] * cx1
    conv = conv + convw_ref[2:3, :] * cx
    cxprev_s[...] = cx[TM_MIX - SUBLANES:, :]
    cxlast_ref[...] = cx[TM_MIX - SUBLANES:, :]
    bx = (proj(OFF_BG, D_MODEL) * conv).astype(BF16)
    sa_rows = []
    for c in range(nblk):
        sa_rows.append(jnp.concatenate(
            [_dot(wsp_s[g], va[c * BLK:(c + 1) * BLK, g * LANES:(g + 1) * LANES])
             for g in range(A_GROUPS)], axis=1) + bsp_ref[...])
    ax = (ua * jnp.concatenate(sa_rows, axis=0)).astype(BF16)

    cos_a, sin_a = cos_a_ref[...], sin_a_ref[...]
    cos_t = cos_a * cos_b_ref[...] - sin_a * sin_b_ref[...]
    sin_s = sin_a * cos_bs_ref[...] + cos_a * sin_bs_ref[...]
    q = (_rope(proj(OFF_Q, N_HEADS * HEAD_DIM), cos_t, sin_s)
         * (LOG2_E * HEAD_DIM ** -0.5)).astype(BF16)
    k = _rope(proj(OFF_K, KV_DIM), cos_t, sin_s)
    v = proj(OFF_V, KV_DIM)
    klast_ref[...] = k[TM_MIX - BLK:, :]
    vlast_ref[...] = v[TM_MIX - BLK:, :]

    merged_ab, gate_c = [], []

    def gated_branch_pieces():
        for j in range(D_MODEL // GATE_COLS):
            cols = slice(j * GATE_COLS, (j + 1) * GATE_COLS)
            y_a = _dot(ax, wa_ref[:, cols])
            yield
            part = jax.nn.sigmoid(proj(OFF_GA + j * GATE_COLS, GATE_COLS)) * y_a
            yield
            y_b = _dot(bx, wb_ref[:, cols])
            yield
            merged_ab.append(part + jax.nn.sigmoid(proj(OFF_GB + j * GATE_COLS, GATE_COLS)) * y_b)
            yield
            gate_c.append(jax.nn.sigmoid(proj(OFF_GC + j * GATE_COLS, GATE_COLS)))
            yield

    pieces = gated_branch_pieces()
    n_pieces = 5 * (D_MODEL // GATE_COLS)
    for _ in range(PIECES_BEFORE_ATTN):
        next(pieces)

    qi = jax.lax.broadcasted_iota(jnp.int32, (BLK, 2 * BLK), 0)
    kj = jax.lax.broadcasted_iota(jnp.int32, (BLK, 2 * BLK), 1)
    band = (kj >= qi) & (kj <= qi + WINDOW)
    lane_low = _lane_iota((BLK, LANES)) < HEAD_DIM

    k_parts = [[(kprev_s[g, 0], kprev_s[g, 1]) for g in range(N_KV_HEADS)]]
    v_parts = [[(vprev_s[g, 0], vprev_s[g, 1]) for g in range(N_KV_HEADS)]]
    for c in range(nblk):
        k_parts.append(_split_kv_heads(k[c * BLK:(c + 1) * BLK, :]))
        v_parts.append(_split_kv_heads(v[c * BLK:(c + 1) * BLK, :]))
    for g in range(N_KV_HEADS):
        kprev_s[g, 0] = k_parts[nblk][g][0]
        kprev_s[g, 1] = k_parts[nblk][g][1]
        vprev_s[g, 0] = v_parts[nblk][g][0]
        vprev_s[g, 1] = v_parts[nblk][g][1]
    mask_first = band & (kj >= jnp.where(step == 0, BLK, 0))

    def block_diag(parts, c, g):
        return jnp.concatenate([parts[c][g][0], parts[c + 1][g][0],
                                parts[c][g][1], parts[c + 1][g][1]], axis=0)

    def scores(c, m):
        kbd = block_diag(k_parts, c, (2 * m) // Q_PER_KV)
        return _dot_nt(q[c * BLK:(c + 1) * BLK, m * LANES:(m + 1) * LANES], kbd)

    def softmax_numerators(c, m, s):
        mask = mask_first if c == 0 else band
        es, invs = [], []
        for hh in range(2):
            sink = sinks_ref[layer, 2 * m + hh] * LOG2_E
            sh = jnp.where(mask, s[:, hh * 2 * BLK:(hh + 1) * 2 * BLK], NEG_INF)
            mx = jnp.maximum(jnp.max(sh, axis=-1, keepdims=True), sink)
            e = jnp.exp2(sh - mx)
            den = jnp.sum(e, axis=-1, keepdims=True) + jnp.exp2(sink - mx)
            es.append(e.astype(BF16))
            invs.append(1.0 / den)
        return jnp.concatenate(es, axis=1), jnp.where(lane_low, invs[0], invs[1])

    def weighted_values(c, m, e, inv):
        vbd = block_diag(v_parts, c, (2 * m) // Q_PER_KV)
        return _dot(e, vbd) * inv

    its = [(c, m) for c in range(nblk) for m in range(N_HEADS // 2)]
    s_queue = [scores(*its[i]) for i in range(min(ATTN_LOOKAHEAD, len(its)))]
    o_parts = {}
    issued = PIECES_BEFORE_ATTN
    for i, (c, m) in enumerate(its):
        if i + ATTN_LOOKAHEAD < len(its):
            s_queue.append(scores(*its[i + ATTN_LOOKAHEAD]))
        e, inv = softmax_numerators(c, m, s_queue.pop(0))
        o_parts[(c, m)] = weighted_values(c, m, e, inv)
        while issued < PIECES_BEFORE_ATTN + ((i + 1) * (n_pieces - PIECES_BEFORE_ATTN)) // len(its):
            next(pieces)
            issued += 1
    o = jnp.concatenate(
        [jnp.concatenate([o_parts[(c, m)] for m in range(N_HEADS // 2)], axis=1)
         for c in range(nblk)], axis=0).astype(BF16)
    y_c = _dot(o, wc_ref[...])
    merged = jnp.concatenate(merged_ab, axis=1) + jnp.concatenate(gate_c, axis=1) * y_c

    merged = merged.astype(BF16)
    for c in range(nblk):
        rows = slice(c * BLK, (c + 1) * BLK)
        x1_ref[rows, :] = x[rows, :] + _rmsnorm(_dot(merged[rows, :], wo_ref[...]), row(gpost_ref))


def _prompt_mixer(layer, x, rope, sinks, gpre, gpost, w_in, lng, lnb, wsp, bsp, convw, convb,
                  wa, wb, wc, wo, cast_layer, cast_weights):
    n = x.shape[0]
    n_steps = n // TM_MIX
    lb = functools.partial(_layer_block, layer)
    casts = [_cast_specs(cast_layer, w, n_steps) for w in cast_weights]
    vec = _whole((DEPTH, D_MODEL))
    in_specs = [
        pl.BlockSpec((TM_MIX, D_MODEL), lambda i, *_: (i, 0)),
        pl.BlockSpec((None, 1, LANES), lambda i, *_: (i, 0, 0)),
        pl.BlockSpec((None, 1, LANES), lambda i, *_: (i, 0, 0)),
        _whole((TM_MIX, LANES)), _whole((TM_MIX, LANES)),
        _whole((TM_MIX, LANES)), _whole((TM_MIX, LANES)),
        vec, vec,
        _whole((D_MODEL, IN_WIDTH)),
        vec, vec,
        lb((A_GROUPS, CHUNK, CHUNK)), lb((CHUNK, D_MODEL)),
        lb((CONV_W, D_MODEL)), vec,
        _whole((D_MODEL, D_MODEL)), _whole((D_MODEL, D_MODEL)),
        _whole((D_MODEL, D_MODEL)), _whole((D_MODEL, D_MODEL)),
    ] + [c[0] for c in casts]
    assert len(in_specs) == N_MIXER_INPUTS + len(casts)
    out_specs = [
        pl.BlockSpec((TM_MIX, D_MODEL), lambda i, *_: (i, 0)),
        pl.BlockSpec((SUBLANES, D_MODEL), lambda i, *_: (0, 0)),
        pl.BlockSpec((BLK, KV_DIM), lambda i, *_: (0, 0)),
        pl.BlockSpec((BLK, KV_DIM), lambda i, *_: (0, 0)),
    ] + [c[1] for c in casts]
    out_shape = [
        jax.ShapeDtypeStruct((n, D_MODEL), F32),
        jax.ShapeDtypeStruct((SUBLANES, D_MODEL), F32),
        jax.ShapeDtypeStruct((BLK, KV_DIM), F32),
        jax.ShapeDtypeStruct((BLK, KV_DIM), F32),
    ] + [c[2] for c in casts]
    scratch = [
        pltpu.VMEM((A_GROUPS, CHUNK, CHUNK), BF16),
        pltpu.VMEM((N_KV_HEADS, 2, BLK, LANES), BF16),
        pltpu.VMEM((N_KV_HEADS, 2, BLK, LANES), BF16),
        pltpu.VMEM((SUBLANES, D_MODEL), F32),
    ]
    return pl.pallas_call(
        functools.partial(_mixer_kernel, layer, len(casts)),
        grid_spec=pltpu.PrefetchScalarGridSpec(
            num_scalar_prefetch=1, grid=(n_steps,), in_specs=in_specs, out_specs=out_specs,
            scratch_shapes=scratch),
        out_shape=out_shape,
        compiler_params=pltpu.CompilerParams(
            dimension_semantics=("arbitrary",), vmem_limit_bytes=V7X_VMEM_LIMIT),
        name="prompt_mixer",
    )(sinks, x, *rope, gpre, gpost, w_in, lng, lnb, wsp, bsp, convw, convb, wa, wb, wc, wo,
      *cast_weights)


def _swiglu(x, gpre, gpost, wg_ref, wu_ref, wd_ref):
    h = _rmsnorm(x, gpre).astype(BF16)
    a = jax.nn.silu(_dot(h, wg_ref[...])) * _dot(h, wu_ref[...])
    return x + _rmsnorm(_dot(a.astype(BF16), wd_ref[...]), gpost)


N_FFN_INPUTS = 13


def _ffn_kernel(layer, n_cast, *refs):
    (x_ref, gpre_ref, gpost_ref, wg_ref, wu_ref, wd_ref,
     xs_ref, os_ref, mab_ref, gc_ref, gpost_mix_ref, wc_ref, wo_ref) = refs[:N_FFN_INPUTS]
    cast_src = refs[N_FFN_INPUTS:N_FFN_INPUTS + n_cast]
    y_ref, ys_ref = refs[N_FFN_INPUTS + n_cast:N_FFN_INPUTS + n_cast + 2]
    cast_dst = refs[N_FFN_INPUTS + n_cast + 2:]
    row = lambda ref: ref[layer:layer + 1, :]
    rows = [slice(r * FFN_ROWS, (r + 1) * FFN_ROWS) for r in range(TM_FFN // FFN_ROWS)]
    xs = [x_ref[r, :] for r in rows]
    acts = []
    for x in xs:
        h = _rmsnorm(x, row(gpre_ref)).astype(BF16)
        acts.append((jax.nn.silu(_dot(h, wg_ref[...])) * _dot(h, wu_ref[...])).astype(BF16))
    for r, x, a in zip(rows, xs, acts):
        y_ref[r, :] = x + _rmsnorm(_dot(a, wd_ref[...]), row(gpost_ref))
    _cast_row_blocks(cast_src, cast_dst)

    @pl.when(pl.program_id(0) == pl.num_programs(0) - 1)
    def _sample_rows():
        y_c = _dot(os_ref[...].astype(BF16), wc_ref[...])
        merged = mab_ref[...] + gc_ref[...] * y_c
        x1 = xs_ref[...] + _rmsnorm(_dot(merged.astype(BF16), wo_ref[...]), row(gpost_mix_ref))
        ys_ref[...] = _swiglu(x1, row(gpre_ref), row(gpost_ref), wg_ref, wu_ref, wd_ref)


def _ffn(layer, x, gpre, gpost, wg, wu, wd, xs, o_s, mab, gc, gpost_mix, wc, wo,
         cast_layer, cast_weights):
    n = x.shape[0]
    n_steps = n // TM_FFN
    casts = [_cast_specs(cast_layer, w, n_steps) for w in cast_weights]
    row_tile = pl.BlockSpec((TM_FFN, D_MODEL), lambda i: (i, 0))
    sample_rows = _whole(xs.shape)
    vec = _whole((DEPTH, D_MODEL))
    return pl.pallas_call(
        functools.partial(_ffn_kernel, layer, len(casts)),
        grid=(n_steps,),
        in_specs=[row_tile, vec, vec,
                  _whole((D_MODEL, D_FF)), _whole((D_MODEL, D_FF)), _whole((D_FF, D_MODEL)),
                  sample_rows, sample_rows, sample_rows, sample_rows, vec,
                  _whole((D_MODEL, D_MODEL)), _whole((D_MODEL, D_MODEL))]
                 + [c[0] for c in casts],
        out_specs=[row_tile, pl.BlockSpec(xs.shape, lambda i: (0, 0))] + [c[1] for c in casts],
        out_shape=[jax.ShapeDtypeStruct((n, D_MODEL), F32), jax.ShapeDtypeStruct(xs.shape, F32)]
                  + [c[2] for c in casts],
        compiler_params=pltpu.CompilerParams(
            dimension_semantics=("arbitrary",), vmem_limit_bytes=V7X_VMEM_LIMIT),
        name="ffn",
    )(x, gpre, gpost, wg, wu, wd, xs, o_s, mab, gc, gpost_mix, wc, wo, *cast_weights)


SAMPLE_SECTIONS = (
    (OFF_UA, D_MODEL), (OFF_VA, D_MODEL), (OFF_GA, D_MODEL), (OFF_CG, D_MODEL), (OFF_HB, D_MODEL),
    (OFF_BG, D_MODEL), (OFF_GB, D_MODEL), (OFF_GC, D_MODEL), (OFF_Q, N_HEADS * HEAD_DIM),
    (OFF_K, KV_DIM), (OFF_V, KV_DIM))


def _sample_proj_kernel(layer, x_ref, hist_ref, cos_ref, sin_ref, gpre_ref, w_in_hbm, lng_ref,
                        lnb_ref, wsp0_ref, bsp0_ref, convw_ref, convb_ref, wa_ref, wb_ref,
                        va_ref, newconv_ref, q_ref, k_ref, v_ref, mab_ref, gc_ref,
                        w_in_s, w_in_sems):
    row = lambda ref: ref[layer:layer + 1, :]

    def section_copy(idx):
        off, width = SAMPLE_SECTIONS[idx]
        cols = pl.ds(off, width)
        return pltpu.make_async_copy(w_in_hbm.at[:, cols], w_in_s.at[:, cols], w_in_sems.at[idx])

    for idx in range(len(SAMPLE_SECTIONS)):
        section_copy(idx).start()

    x = x_ref[...]
    h = _rmsnorm(x, row(gpre_ref)).astype(BF16)

    def proj(off, width):
        section_copy(SAMPLE_SECTIONS.index((off, width))).wait()
        return _dot(h, w_in_s[:, off:off + width])

    ua = jax.nn.gelu(proj(OFF_UA, D_MODEL))
    va = _layernorm(jax.nn.gelu(proj(OFF_VA, D_MODEL)), row(lng_ref), row(lnb_ref))
    va_ref[...] = va
    sa = row(wsp0_ref) * va + row(bsp0_ref)
    y_a = _dot((ua * sa).astype(BF16), wa_ref[...])
    merged = jax.nn.sigmoid(proj(OFF_GA, D_MODEL)) * y_a

    cx = proj(OFF_CG, D_MODEL) * proj(OFF_HB, D_MODEL)
    h0 = hist_ref[:, 0:D_MODEL]
    h1 = hist_ref[:, D_MODEL:2 * D_MODEL]
    conv = row(convb_ref) + convw_ref[0:1, :] * h0
    conv = conv + convw_ref[1:2, :] * h1
    conv = conv + convw_ref[2:3, :] * cx
    newconv_ref[:, 0:D_MODEL] = h1
    newconv_ref[:, D_MODEL:2 * D_MODEL] = cx
    y_b = _dot((proj(OFF_BG, D_MODEL) * conv).astype(BF16), wb_ref[...])
    mab_ref[...] = merged + jax.nn.sigmoid(proj(OFF_GB, D_MODEL)) * y_b
    gc_ref[...] = jax.nn.sigmoid(proj(OFF_GC, D_MODEL))

    cos_t = cos_ref[...]
    sin_s = sin_ref[...]
    q_ref[...] = _rope(proj(OFF_Q, N_HEADS * HEAD_DIM), cos_t, sin_s) * (HEAD_DIM ** -0.5)
    k_ref[...] = _rope(proj(OFF_K, KV_DIM), cos_t, sin_s)
    v_ref[...] = proj(OFF_V, KV_DIM)


def _sample_proj(layer, x, hist, cos_t, sin_s, gpre, w_in, lng, lnb, wsp0, bsp0, convw, convb,
                 wa, wb):
    b = x.shape[0]
    lb = functools.partial(_layer_block, layer)
    vec = _whole((DEPTH, D_MODEL))
    in_specs = [
        _whole(x.shape), lb((b, (CONV_W - 1) * D_MODEL)), _whole(cos_t.shape), _whole(sin_s.shape),
        vec, pl.BlockSpec(memory_space=pl.ANY), vec, vec,
        vec, vec, lb((CONV_W, D_MODEL)), vec,
        _whole((D_MODEL, D_MODEL)), _whole((D_MODEL, D_MODEL)),
    ]
    scratch = [pltpu.VMEM((D_MODEL, IN_WIDTH), BF16),
               pltpu.SemaphoreType.DMA((len(SAMPLE_SECTIONS),))]
    out_shape = [
        jax.ShapeDtypeStruct((b, D_MODEL), F32),
        jax.ShapeDtypeStruct((b, 2 * D_MODEL), F32),
        jax.ShapeDtypeStruct((b, D_MODEL), F32),
        jax.ShapeDtypeStruct((b, KV_DIM), F32),
        jax.ShapeDtypeStruct((b, KV_DIM), F32),
        jax.ShapeDtypeStruct((b, D_MODEL), F32),
        jax.ShapeDtypeStruct((b, D_MODEL), F32),
    ]
    return pl.pallas_call(
        functools.partial(_sample_proj_kernel, layer),
        grid=(1,),
        in_specs=in_specs,
        out_specs=[pl.BlockSpec(s.shape, lambda i: (0, 0)) for s in out_shape],
        out_shape=out_shape,
        scratch_shapes=scratch,
        compiler_params=pltpu.CompilerParams(
            dimension_semantics=("arbitrary",), vmem_limit_bytes=V7X_VMEM_LIMIT),
        name="sample_proj",
    )(x, hist, cos_t, sin_s, gpre, w_in, lng, lnb, wsp0, bsp0, convw, convb, wa, wb)


def _sample_attn_kernel(layer, n_aliased, sinks_ref, *refs):
    (q_ref, knew_ref, vnew_ref, knew_t_ref, vnew_t_ref, kc_ref, vc_ref,
     o_ref, kout_ref, vout_ref) = refs[n_aliased:]
    step = pl.program_id(0)
    qf = q_ref[...]
    s = jnp.einsum('grd,gdk->grk', qf.astype(BF16), kc_ref[...].astype(BF16),
                   preferred_element_type=F32)
    s_new = jnp.sum(qf * knew_ref[...], axis=-1, keepdims=True)
    gidx = jax.lax.broadcasted_iota(jnp.int32, (S2_GROUPS, Q_PER_KV, 1), 0)
    ridx = jax.lax.broadcasted_iota(jnp.int32, (S2_GROUPS, Q_PER_KV, 1), 1)
    head = (gidx % N_KV_HEADS) * Q_PER_KV + ridx
    sink = jnp.zeros((S2_GROUPS, Q_PER_KV, 1), F32)
    for hh in range(N_HEADS):
        sink = jnp.where(head == hh, sinks_ref[layer, hh], sink)
    mx = jnp.maximum(jnp.maximum(jnp.max(s, axis=-1, keepdims=True), s_new), sink)
    e = jnp.exp(s - mx)
    e_new = jnp.exp(s_new - mx)
    den = jnp.sum(e, axis=-1, keepdims=True) + e_new + jnp.exp(sink - mx)
    o = jnp.einsum('grk,gdk->grd', e.astype(BF16), vc_ref[...].astype(BF16),
                   preferred_element_type=F32)
    o_ref[...] = (o + e_new * vnew_ref[...]) * (1.0 / den)

    steps_per_tile = LANES // S2_GROUPS
    tile = pl.multiple_of((step // steps_per_tile) * LANES, LANES)
    first_col = (step % steps_per_tile) * S2_GROUPS
    last_lane = _lane_iota((HEAD_DIM, WINDOW)) == WINDOW - 1
    for new_t_ref, cache_ref, out_ref in ((knew_t_ref, kc_ref, kout_ref),
                                          (vnew_t_ref, vc_ref, vout_ref)):
        new_cols = new_t_ref[:, pl.ds(tile, LANES)]
        for j in range(S2_GROUPS):
            col = pltpu.roll(new_cols, WINDOW - 1 - first_col - j, axis=1)
            out_ref[j] = jnp.where(last_lane, col, pltpu.roll(cache_ref[j], WINDOW - 1, axis=1))


def _sample_attn(layer, sinks, q, knew, vnew, knew_t, vnew_t, kcache, vcache, kout_prev, vout_prev):
    groups = q.shape[0]
    group3 = lambda d1, d2: pl.BlockSpec((S2_GROUPS, d1, d2), lambda i, *_: (i, 0, 0))
    cache_blk = pl.BlockSpec((None, S2_GROUPS, HEAD_DIM, WINDOW), lambda i, *_: (layer, i, 0, 0))
    aliased = [] if kout_prev is None else [kout_prev, vout_prev]
    n_al = len(aliased)
    in_specs = ([pl.BlockSpec(memory_space=pl.ANY)] * n_al +
                [group3(Q_PER_KV, HEAD_DIM), group3(1, HEAD_DIM), group3(1, HEAD_DIM),
                 _whole(knew_t.shape), _whole(vnew_t.shape), cache_blk, cache_blk])
    io_alias = {1: 1, 2: 2} if n_al else {}
    return pl.pallas_call(
        functools.partial(_sample_attn_kernel, layer, n_al),
        grid_spec=pltpu.PrefetchScalarGridSpec(
            num_scalar_prefetch=1, grid=(groups // S2_GROUPS,),
            in_specs=in_specs,
            out_specs=[group3(Q_PER_KV, HEAD_DIM), cache_blk, cache_blk]),
        out_shape=[jax.ShapeDtypeStruct((groups, Q_PER_KV, HEAD_DIM), F32),
                   jax.ShapeDtypeStruct(kcache.shape, F32),
                   jax.ShapeDtypeStruct(vcache.shape, F32)],
        input_output_aliases=io_alias,
        compiler_params=pltpu.CompilerParams(
            dimension_semantics=("arbitrary",), vmem_limit_bytes=V7X_VMEM_LIMIT),
        name="sample_attn",
    )(sinks, *aliased, q, knew, vnew, knew_t, vnew_t, kcache, vcache)


def _rope_angles(pos):
    half = HEAD_DIM // 2
    inv = jnp.power(jnp.float32(ROPE_THETA), -jnp.arange(half, dtype=F32) * (2.0 / HEAD_DIM))
    return pos.astype(F32)[:, None] * jnp.tile(inv, LANES // half)[None, :]


def _rope_sign():
    lane = jnp.arange(LANES, dtype=jnp.int32)
    return jnp.where((lane % HEAD_DIM) < (HEAD_DIM // 2), -1.0, 1.0).astype(F32)[None, :]


def kernel(x_prompt, x_sample, state_conv, cache_win_k, cache_win_v, norm_pre_mix, norm_post_mix,
           norm_pre_ffn, norm_post_ffn, w_in, chunk_ln_g, chunk_ln_b, w_spatial, b_spatial, conv_w,
           conv_b, attn_sinks, w_br_a, w_br_b, w_br_c, w_out, w_ffn_gate, w_ffn_up, w_ffn_down):
    xp = x_prompt.reshape(SEQ, D_MODEL)
    xs = x_sample.reshape(DEC_BATCH, D_MODEL)

    sign = _rope_sign()
    ang_a = _rope_angles(jnp.arange(SEQ // TM_MIX, dtype=jnp.int32) * TM_MIX)
    ang_b = _rope_angles(jnp.arange(TM_MIX, dtype=jnp.int32))
    rope_p = (jnp.cos(ang_a)[:, None, :], jnp.sin(ang_a)[:, None, :],
              jnp.cos(ang_b), jnp.sin(ang_b), sign * jnp.cos(ang_b), sign * jnp.sin(ang_b))
    ang_s = _rope_angles(jnp.full((1,), PAST_LEN, dtype=jnp.int32))
    cos_s, sin_s = jnp.cos(ang_s), sign * jnp.sin(ang_s)

    mixer_f32 = (w_in, w_br_a, w_br_b, w_br_c, w_out)
    ffn_f32 = (w_ffn_gate, w_ffn_up, w_ffn_down)
    mixer_w = tuple(w[0].astype(BF16) for w in mixer_f32)
    gpre, gpost, gpre2, gpost2 = norm_pre_mix, norm_post_mix, norm_pre_ffn, norm_post_ffn
    lng, lnb, convb = chunk_ln_g, chunk_ln_b, conv_b
    bsp = jnp.repeat(jnp.transpose(b_spatial, (0, 2, 1)), A_GROUP_DIM, axis=2)
    wsp0 = jnp.repeat(w_spatial[:, :, 0, 0], A_GROUP_DIM, axis=1)
    bsp0 = jnp.repeat(b_spatial[:, :, 0], A_GROUP_DIM, axis=1)
    hist = state_conv.reshape(DEPTH, DEC_BATCH, (CONV_W - 1) * D_MODEL)
    n_groups = DEC_BATCH * N_KV_HEADS
    to_groups = lambda c: jnp.transpose(c, (0, 1, 3, 4, 2)).reshape(DEPTH, n_groups, HEAD_DIM, WINDOW)
    kcache, vcache = to_groups(cache_win_k), to_groups(cache_win_v)

    p_conv, p_k, p_v = [], [], []
    s_conv, s_cv = [], []
    knext = vnext = None
    for l in range(DEPTH):
        w_in_b, wa, wb, wc, wo = mixer_w
        x1, cxlast, klast, vlast, wg, wu, wd = _prompt_mixer(
            l, xp, rope_p, attn_sinks, gpre, gpost, w_in_b, lng, lnb, w_spatial, bsp,
            conv_w, convb, wa, wb, wc, wo, l, ffn_f32)
        p_conv.append(cxlast[SUBLANES - (CONV_W - 1):].reshape(1, CONV_W - 1, D_MODEL))
        p_k.append(klast.reshape(1, WINDOW, N_KV_HEADS, HEAD_DIM))
        p_v.append(vlast.reshape(1, WINDOW, N_KV_HEADS, HEAD_DIM))

        va, newconv, q, knew, vnew, mab, gc = _sample_proj(
            l, xs, hist, cos_s, sin_s, gpre, w_in_b, lng, lnb, wsp0, bsp0, conv_w, convb, wa, wb)
        knew = knew.reshape(n_groups, HEAD_DIM)
        vnew = vnew.reshape(n_groups, HEAD_DIM)
        o, knext, vnext = _sample_attn(
            l, attn_sinks, q.reshape(n_groups, Q_PER_KV, HEAD_DIM),
            knew[:, None, :], vnew[:, None, :], jnp.transpose(knew), jnp.transpose(vnew),
            kcache, vcache, knext, vnext)
        s_conv.append(newconv.reshape(DEC_BATCH, CONV_W - 1, D_MODEL))
        s_cv.append(va.reshape(DEC_BATCH, 1, D_MODEL))

        cast_next = (l + 1, mixer_f32) if l + 1 < DEPTH else (0, ())
        xp, xs, *mixer_w = _ffn(l, x1, gpre2, gpost2, wg, wu, wd,
                                xs, o.reshape(DEC_BATCH, D_MODEL), mab, gc, gpost, wc, wo,
                                *cast_next)

    from_groups = lambda c: jnp.transpose(
        c.reshape(DEPTH, DEC_BATCH, N_KV_HEADS, HEAD_DIM, WINDOW), (0, 1, 4, 2, 3))
    return (xp.reshape(1, SEQ, D_MODEL), xs.reshape(DEC_BATCH, 1, D_MODEL),
            jnp.stack(p_conv), jnp.stack(p_k), jnp.stack(p_v),
            jnp.stack(s_conv), from_groups(knext), from_groups(vnext), jnp.stack(s_cv))
```
